```python
import jax, jax.numpy as jnp
from jax import lax
import numpy as np

D_MODEL = 1024
BATCH = 8
SEQ = 4096
DEPTH = 4

N_MIXERS = 4
D_FF = ((8 * D_MODEL // 3 + 127) // 128) * 128
PLE_DIM = 256
CHUNK = 128
A_HALF = 3 * D_MODEL
A_GROUPS = A_HALF // 256
POOL_WINDOWS = (2, 4, 8, 16)
POOL_GROUP = D_MODEL // len(POOL_WINDOWS)
CONV_WIDTH = 31
SHORT_CONV_WIDTH = 3
EPS = 1e-6

kernel_name = "hybrid_interleaved_gmlp_pool_conformer_shortconv"


def n_layers_of(m):
    return len(range(m, DEPTH, N_MIXERS))


def rms_norm(x, g):
    xf = x.astype(jnp.float32)
    y = xf * lax.rsqrt(jnp.mean(xf * xf, axis=-1, keepdims=True) + EPS)
    return (y * g.astype(jnp.float32)).astype(x.dtype)


def layer_norm(x, g, b):
    xf = x.astype(jnp.float32)
    mu = jnp.mean(xf, axis=-1, keepdims=True)
    xc = xf - mu
    y = xc * lax.rsqrt(jnp.mean(xc * xc, axis=-1, keepdims=True) + EPS)
    return (y * g.astype(jnp.float32) + b.astype(jnp.float32)).astype(x.dtype)


def swiglu(x, w_gate, w_up, w_down):
    return (jax.nn.silu(x @ w_gate) * (x @ w_up)) @ w_down


def causal_depthwise_conv(x, w, b=None):
    k, c = w.shape
    y = lax.conv_general_dilated(
        x, w[:, None, :].astype(x.dtype), window_strides=(1,),
        padding=((k - 1, 0),), dimension_numbers=('NWC', 'WIO', 'NWC'),
        feature_group_count=c)
    return y if b is None else y + b


def mixer_gmlp_chunked(h, w_in, v_g, v_b, w_s, b_s, w_out):
    B, S, _ = h.shape
    z = jax.nn.gelu(h @ w_in)
    u, v = jnp.split(z, 2, axis=-1)
    v = layer_norm(v, v_g, v_b)
    n_chunks = S // CHUNK
    v = v.reshape(B, n_chunks, CHUNK, A_GROUPS, A_HALF // A_GROUPS)
    mask = jnp.tril(jnp.ones((CHUNK, CHUNK), dtype=bool))
    ws = jnp.where(mask[None], w_s, jnp.zeros((), w_s.dtype)).astype(v.dtype)
    sv = jnp.einsum('gts,bnsgc->bntgc', ws, v) + b_s.T[None, None, :, :, None]
    y = u * sv.reshape(B, S, A_HALF)
    return y @ w_out


def mixer_multiscale_pool(h, w_grp, scale):
    B, S, D = h.shape
    hf = h.astype(jnp.float32)
    cs = jnp.cumsum(hf, axis=1)
    t = jnp.arange(1, S + 1, dtype=jnp.float32)[None, :, None]
    outs = []
    for g, w in enumerate(POOL_WINDOWS):
        sl = slice(g * POOL_GROUP, (g + 1) * POOL_GROUP)
        c = cs[..., sl]
        prev = jnp.pad(c, ((0, 0), (w, 0), (0, 0)))[:, :S]
        mean = (c - prev) / jnp.minimum(t, w)
        outs.append(mean - hf[..., sl])
    pooled = jnp.stack(outs, axis=2).astype(h.dtype)
    y = jnp.einsum('bsgc,gcd->bsgd', pooled, w_grp).reshape(B, S, D)
    return y * scale


def mixer_conformer_conv(h, w_pw1, w_dw, b_dw, n_g, n_b, w_pw2):
    a, g = jnp.split(h @ w_pw1, 2, axis=-1)
    z = a * jax.nn.sigmoid(g)
    z = causal_depthwise_conv(z, w_dw, b_dw)
    z = jax.nn.silu(layer_norm(z, n_g, n_b))
    return z @ w_pw2


def mixer_short_gated_conv(h, w_in, w_conv, w_out):
    bg, cg, xv = jnp.split(h @ w_in, 3, axis=-1)
    return (bg * causal_depthwise_conv(cg * xv, w_conv)) @ w_out


def setup_inputs(seed: int = 0) -> dict:
    key = jax.random.key(seed)
    keys = iter(jax.random.split(key, 64))
    f32 = jnp.float32

    def nrm(shape, scale):
        return jax.random.normal(next(keys), shape, f32) * scale

    def dense(shape, fan_in):
        return nrm(shape, fan_in ** -0.5)

    def gain(shape):
        return 1.0 + nrm(shape, 0.05)

    L, D, F = DEPTH, D_MODEL, D_FF
    nA, nB, nC, nD = n_layers_of(0), n_layers_of(1), n_layers_of(2), n_layers_of(3)
    Gc = A_HALF // A_GROUPS
    return {
        "x": nrm((BATCH, SEQ, D), 1.0),
        "p": nrm((DEPTH, BATCH, SEQ, PLE_DIM), 1.0),
        "ff1_pre_g": gain((L, D)),
        "ff1_w_gate": dense((L, D, F), D),
        "ff1_w_up": dense((L, D, F), D),
        "ff1_w_down": dense((L, F, D), F),
        "ff1_post_g": gain((L, D)),
        "mix_pre_g": gain((L, D)),
        "mix_post_g": gain((L, D)),
        "ff2_pre_g": gain((L, D)),
        "ff2_w_gate": dense((L, D, F), D),
        "ff2_w_up": dense((L, D, F), D),
        "ff2_w_down": dense((L, F, D), F),
        "ff2_post_g": gain((L, D)),
        "ple_gate_norm_g": gain((L, D)),
        "ple_w_gate": dense((L, D, D), D),
        "ple_w_proj": dense((L, PLE_DIM, D), PLE_DIM),
        "ple_post_g": gain((L, D)),
        "a_w_in": dense((nA, D, 2 * A_HALF), D),
        "a_v_norm_g": gain((nA, A_HALF)),
        "a_v_norm_b": nrm((nA, A_HALF), 0.02),
        "a_w_s": dense((nA, A_GROUPS, CHUNK, CHUNK), CHUNK),
        "a_b_s": 1.0 + nrm((nA, A_GROUPS, CHUNK), 0.05),
        "a_w_out": dense((nA, A_HALF, D), A_HALF),
        "b_w_grp": dense((nB, len(POOL_WINDOWS), POOL_GROUP, POOL_GROUP), POOL_GROUP),
        "b_scale": gain((nB, D)) + nrm((nB, D), 0.05),
        "c_w_pw1": dense((nC, D, 2 * D), D),
        "c_w_dw": dense((nC, CONV_WIDTH, D), CONV_WIDTH),
        "c_b_dw": nrm((nC, D), 0.02),
        "c_norm_g": gain((nC, D)),
        "c_norm_b": nrm((nC, D), 0.02),
        "c_w_pw2": dense((nC, D, D), D),
        "d_w_in": dense((nD, D, 3 * D), D),
        "d_w_conv": dense((nD, SHORT_CONV_WIDTH, D), SHORT_CONV_WIDTH),
        "d_w_out": dense((nD, D, D), D),
    }


def reference(x, p,
              ff1_pre_g, ff1_w_gate, ff1_w_up, ff1_w_down, ff1_post_g,
              mix_pre_g, mix_post_g,
              ff2_pre_g, ff2_w_gate, ff2_w_up, ff2_w_down, ff2_post_g,
              ple_gate_norm_g, ple_w_gate, ple_w_proj, ple_post_g,
              a_w_in, a_v_norm_g, a_v_norm_b, a_w_s, a_b_s, a_w_out,
              b_w_grp, b_scale,
              c_w_pw1, c_w_dw, c_b_dw, c_norm_g, c_norm_b, c_w_pw2,
              d_w_in, d_w_conv, d_w_out):
    h = x
    for i in range(DEPTH):
        f = swiglu(rms_norm(h, ff1_pre_g[i]), ff1_w_gate[i], ff1_w_up[i], ff1_w_down[i])
        h = h + 0.5 * rms_norm(f, ff1_post_g[i])
        hn = rms_norm(h, mix_pre_g[i])
        m, j = i % N_MIXERS, i // N_MIXERS
        if m == 0:
            y = mixer_gmlp_chunked(hn, a_w_in[j], a_v_norm_g[j], a_v_norm_b[j],
                                   a_w_s[j], a_b_s[j], a_w_out[j])
        elif m == 1:
            y = mixer_multiscale_pool(hn, b_w_grp[j], b_scale[j])
        elif m == 2:
            y = mixer_conformer_conv(hn, c_w_pw1[j], c_w_dw[j], c_b_dw[j],
                                     c_norm_g[j], c_norm_b[j], c_w_pw2[j])
        else:
            y = mixer_short_gated_conv(hn, d_w_in[j], d_w_conv[j], d_w_out[j])
        h = h + rms_norm(y, mix_post_g[i])
        f = swiglu(rms_norm(h, ff2_pre_g[i]), ff2_w_gate[i], ff2_w_up[i], ff2_w_down[i])
        h = h + 0.5 * rms_norm(f, ff2_post_g[i])
        gate = jax.nn.sigmoid(rms_norm(h, ple_gate_norm_g[i]) @ ple_w_gate[i])
        e = (p[i].astype(h.dtype) @ ple_w_proj[i]) * gate
        h = h + rms_norm(e, ple_post_g[i])
    return h
```

```python
import functools

import jax
import jax.numpy as jnp
from jax import lax
from jax.experimental import pallas as pl
from jax.experimental.pallas import tpu as pltpu

D_MODEL = 1024
BATCH = 8
SEQ = 4096
DEPTH = 4
N_MIXERS = 4
D_FF = 2816
PLE_DIM = 256
CHUNK = 128
A_HALF = 3 * D_MODEL
A_GROUPS = A_HALF // 256
A_GROUP_WIDTH = A_HALF // A_GROUPS
POOL_WINDOWS = (2, 4, 8, 16)
POOL_GROUP = D_MODEL // len(POOL_WINDOWS)
CONV_WIDTH = 31
SHORT_CONV_WIDTH = 3
EPS = 1e-6

TOKENS = BATCH * SEQ
ROW_TILE = 512
TILES_PER_SEQ = SEQ // ROW_TILE
FF_CHUNK = 256
SUBLANES = 8
POOL_HALO = 16
CONV_HALO = 32
SHORT_HALO = 8
VMEM_LIMIT_BYTES = 56 * 1024 * 1024

BF16 = jnp.bfloat16
F32 = jnp.float32


def _dot(a, b):
    return jnp.dot(a, b, preferred_element_type=F32)


def _rms(x, g):
    return x * lax.rsqrt(jnp.mean(x * x, axis=-1, keepdims=True) + EPS) * g


def _layer_norm(x, g, b):
    mu = jnp.mean(x, axis=-1, keepdims=True)
    xc = x - mu
    return xc * lax.rsqrt(jnp.mean(xc * xc, axis=-1, keepdims=True) + EPS) * g + b


def _swiglu_tile(xn, wg_ref, wu_ref, wd_ref):
    acc = jnp.zeros((xn.shape[0], D_MODEL), F32)
    for c in range(D_FF // FF_CHUNK):
        cols = slice(c * FF_CHUNK, (c + 1) * FF_CHUNK)
        gate = _dot(xn, wg_ref[:, cols])
        up = _dot(xn, wu_ref[:, cols])
        hid = (gate * jax.nn.sigmoid(gate) * up).astype(BF16)
        acc = acc + _dot(hid, wd_ref[cols, :])
    return acc


def _ffn_kernel(x_ref, pre_g, wg, wu, wd, post_g, o_ref):
    x = x_ref[...]
    xn = _rms(x, pre_g[...]).astype(BF16)
    f = _swiglu_tile(xn, wg, wu, wd)
    o_ref[...] = x + 0.5 * _rms(f, post_g[...])


def _ffn_ple_kernel(x_ref, p_ref, pre_g, wg, wu, wd, post_g,
                    gn_g, w_gate, w_proj, ple_post_g, o_ref):
    x = x_ref[...]
    xn = _rms(x, pre_g[...]).astype(BF16)
    f = _swiglu_tile(xn, wg, wu, wd)
    h = x + 0.5 * _rms(f, post_g[...])
    gate = jax.nn.sigmoid(_dot(_rms(h, gn_g[...]).astype(BF16), w_gate[...]))
    e = _dot(p_ref[...].astype(BF16), w_proj[...]) * gate
    o_ref[...] = h + _rms(e, ple_post_g[...])


def _gmlp_kernel(x_ref, pre_g, w_in, v_g, v_b, w_s, b_s, w_out, post_g, o_ref, v_scr):
    x = x_ref[...]
    hn = _rms(x, pre_g[...]).astype(BF16)
    for g in range(A_GROUPS):
        cols = slice(A_HALF + g * A_GROUP_WIDTH, A_HALF + (g + 1) * A_GROUP_WIDTH)
        v_scr[:, g * A_GROUP_WIDTH:(g + 1) * A_GROUP_WIDTH] = jax.nn.gelu(_dot(hn, w_in[:, cols]))
    vn = _layer_norm(v_scr[...], v_g[...], v_b[...]).astype(BF16)
    row = lax.broadcasted_iota(jnp.int32, (CHUNK, CHUNK), 0)
    col = lax.broadcasted_iota(jnp.int32, (CHUNK, CHUNK), 1)
    causal = col <= row
    acc = jnp.zeros((ROW_TILE, D_MODEL), F32)
    for g in range(A_GROUPS):
        cols = slice(g * A_GROUP_WIDTH, (g + 1) * A_GROUP_WIDTH)
        u = jax.nn.gelu(_dot(hn, w_in[:, cols]))
        ws = jnp.where(causal, w_s[g], 0.0).astype(BF16)
        bias = b_s[g]
        sv = jnp.concatenate(
            [_dot(ws, vn[n * CHUNK:(n + 1) * CHUNK, cols]) + bias
             for n in range(ROW_TILE // CHUNK)], axis=0)
        acc = acc + _dot((u * sv).astype(BF16), w_out[cols, :])
    o_ref[...] = x + _rms(acc, post_g[...])


def _seq_tile_start():
    return pl.program_id(0) % TILES_PER_SEQ == 0


def _pool_kernel(x_ref, pre_g, w_grp, scale, post_g, o_ref, ext):
    x = x_ref[...]
    hn = _rms(x, pre_g[...])

    @pl.when(_seq_tile_start())
    def _():
        ext[0:POOL_HALO, :] = jnp.zeros((POOL_HALO, D_MODEL), F32)

    ext[POOL_HALO:POOL_HALO + ROW_TILE, :] = hn
    pos = (pl.program_id(0) % TILES_PER_SEQ) * ROW_TILE + lax.broadcasted_iota(
        jnp.int32, (ROW_TILE, 1), 0)
    outs = []
    for g, w in enumerate(POOL_WINDOWS):
        cols = slice(g * POOL_GROUP, (g + 1) * POOL_GROUP)
        total = hn[:, cols]
        for j in range(1, w):
            total = total + ext[POOL_HALO - j:POOL_HALO - j + ROW_TILE, cols]
        count = jnp.minimum(pos + 1, w).astype(F32)
        pooled = (total / count - hn[:, cols]).astype(BF16)
        outs.append(_dot(pooled, w_grp[g]))
    y = jnp.concatenate(outs, axis=-1) * scale[...]
    o_ref[...] = x + _rms(y, post_g[...])
    ext[0:POOL_HALO, :] = ext[ROW_TILE:ROW_TILE + POOL_HALO, :]


def _conformer_kernel(x_ref, pre_g, w_pw1, w_dw, b_dw, n_g, n_b, w_pw2, post_g, o_ref, ext):
    x = x_ref[...]
    hn = _rms(x, pre_g[...]).astype(BF16)
    a = _dot(hn, w_pw1[:, 0:D_MODEL])
    gate = _dot(hn, w_pw1[:, D_MODEL:2 * D_MODEL])

    @pl.when(_seq_tile_start())
    def _():
        ext[0:CONV_HALO, :] = jnp.zeros((CONV_HALO, D_MODEL), F32)

    ext[CONV_HALO:CONV_HALO + ROW_TILE, :] = a * jax.nn.sigmoid(gate)
    first = CONV_HALO - (CONV_WIDTH - 1)
    conv = jnp.zeros((ROW_TILE, D_MODEL), F32) + b_dw[...]
    for k in range(CONV_WIDTH):
        conv = conv + ext[first + k:first + k + ROW_TILE, :] * w_dw[k:k + 1, :]
    z = _layer_norm(conv, n_g[...], n_b[...])
    z = (z * jax.nn.sigmoid(z)).astype(BF16)
    y = _dot(z, w_pw2[...])
    o_ref[...] = x + _rms(y, post_g[...])
    ext[0:CONV_HALO, :] = ext[ROW_TILE:ROW_TILE + CONV_HALO, :]


def _short_conv_kernel(x_ref, pre_g, w_in, w_conv, w_out, post_g, o_ref, ext):
    x = x_ref[...]
    hn = _rms(x, pre_g[...]).astype(BF16)
    bg = _dot(hn, w_in[:, 0:D_MODEL])
    cg = _dot(hn, w_in[:, D_MODEL:2 * D_MODEL])
    xv = _dot(hn, w_in[:, 2 * D_MODEL:3 * D_MODEL])

    @pl.when(_seq_tile_start())
    def _():
        ext[0:SHORT_HALO, :] = jnp.zeros((SHORT_HALO, D_MODEL), F32)

    ext[SHORT_HALO:SHORT_HALO + ROW_TILE, :] = cg * xv
    first = SHORT_HALO - (SHORT_CONV_WIDTH - 1)
    conv = jnp.zeros((ROW_TILE, D_MODEL), F32)
    for k in range(SHORT_CONV_WIDTH):
        conv = conv + ext[first + k:first + k + ROW_TILE, :] * w_conv[k:k + 1, :]
    y = _dot((bg * conv).astype(BF16), w_out[...])
    o_ref[...] = x + _rms(y, post_g[...])
    ext[0:SHORT_HALO, :] = ext[ROW_TILE:ROW_TILE + SHORT_HALO, :]


def _row_spec(width):
    return pl.BlockSpec((ROW_TILE, width), lambda i: (i, 0))


def _resident_spec(arr):
    zeros = (0,) * arr.ndim
    return pl.BlockSpec(arr.shape, lambda i: zeros, pipeline_mode=pl.Buffered(1))


def _call(body, name, row_inputs, resident_inputs, scratch_shapes=()):
    in_specs = [_row_spec(a.shape[-1]) for a in row_inputs]
    in_specs += [_resident_spec(a) for a in resident_inputs]
    return pl.pallas_call(
        body,
        name=name,
        grid=(TOKENS // ROW_TILE,),
        in_specs=in_specs,
        out_specs=_row_spec(D_MODEL),
        out_shape=jax.ShapeDtypeStruct((TOKENS, D_MODEL), F32),
        scratch_shapes=list(scratch_shapes),
        compiler_params=pltpu.CompilerParams(
            dimension_semantics=("arbitrary",),
            vmem_limit_bytes=VMEM_LIMIT_BYTES),
    )(*row_inputs, *resident_inputs)


def _row_vec(v):
    return v.reshape(1, -1)


def kernel(x, p, ff1_pre_g, ff1_w_gate, ff1_w_up, ff1_w_down, ff1_post_g, mix_pre_g, mix_post_g, ff2_pre_g, ff2_w_gate, ff2_w_up, ff2_w_down, ff2_post_g, ple_gate_norm_g, ple_w_gate, ple_w_proj, ple_post_g, a_w_in, a_v_norm_g, a_v_norm_b, a_w_s, a_b_s, a_w_out, b_w_grp, b_scale, c_w_pw1, c_w_dw, c_b_dw, c_norm_g, c_norm_b, c_w_pw2, d_w_in, d_w_conv, d_w_out):
    bf = lambda w: w.astype(BF16)
    ff1_w_gate, ff1_w_up, ff1_w_down = bf(ff1_w_gate), bf(ff1_w_up), bf(ff1_w_down)
    ff2_w_gate, ff2_w_up, ff2_w_down = bf(ff2_w_gate), bf(ff2_w_up), bf(ff2_w_down)
    ple_w_gate, ple_w_proj = bf(ple_w_gate), bf(ple_w_proj)
    a_w_in, a_w_out, b_w_grp = bf(a_w_in), bf(a_w_out), bf(b_w_grp)
    c_w_pw1, c_w_pw2, d_w_in, d_w_out = bf(c_w_pw1), bf(c_w_pw2), bf(d_w_in), bf(d_w_out)

    h = x.reshape(TOKENS, D_MODEL)
    p = p.reshape(DEPTH, TOKENS, PLE_DIM)
    for i in range(DEPTH):
        h = _call(_ffn_kernel, f"ffn1_l{i}", [h],
                  [_row_vec(ff1_pre_g[i]), ff1_w_gate[i], ff1_w_up[i], ff1_w_down[i],
                   _row_vec(ff1_post_g[i])])
        m, j = i % N_MIXERS, i // N_MIXERS
        pre, post = _row_vec(mix_pre_g[i]), _row_vec(mix_post_g[i])
        if m == 0:
            h = _call(_gmlp_kernel, f"gmlp_l{i}", [h],
                      [pre, a_w_in[j], _row_vec(a_v_norm_g[j]), _row_vec(a_v_norm_b[j]),
                       a_w_s[j], a_b_s[j].reshape(A_GROUPS, CHUNK, 1), a_w_out[j], post],
                      [pltpu.VMEM((ROW_TILE, A_HALF), F32)])
        elif m == 1:
            h = _call(_pool_kernel, f"pool_l{i}", [h],
                      [pre, b_w_grp[j], _row_vec(b_scale[j]), post],
                      [pltpu.VMEM((POOL_HALO + ROW_TILE, D_MODEL), F32)])
        elif m == 2:
            h = _call(_conformer_kernel, f"conformer_l{i}", [h],
                      [pre, c_w_pw1[j], c_w_dw[j], _row_vec(c_b_dw[j]), _row_vec(c_norm_g[j]),
                       _row_vec(c_norm_b[j]), c_w_pw2[j], post],
                      [pltpu.VMEM((CONV_HALO + ROW_TILE, D_MODEL), F32)])
        else:
            h = _call(_short_conv_kernel, f"shortconv_l{i}", [h],
                      [pre, d_w_in[j], d_w_conv[j], d_w_out[j], post],
                      [pltpu.VMEM((SHORT_HALO + ROW_TILE, D_MODEL), F32)])
        h = _call(_ffn_ple_kernel, f"ffn2_ple_l{i}", [h, p[i]],
                  [_row_vec(ff2_pre_g[i]), ff2_w_gate[i], ff2_w_up[i], ff2_w_down[i],
                   _row_vec(ff2_post_g[i]), _row_vec(ple_gate_norm_g[i]), ple_w_gate[i],
                   ple_w_proj[i], _row_vec(ple_post_g[i])])
    return h.reshape(BATCH, SEQ, D_MODEL)
```

```python
import functools

import jax
import jax.numpy as jnp
from jax import lax
from jax.experimental import pallas as pl
from jax.experimental.pallas import tpu as pltpu

D_MODEL = 1024
BATCH = 8
SEQ = 4096
DEPTH = 4
N_MIXERS = 4
D_FF = 2816
PLE_DIM = 256
CHUNK = 128
A_HALF = 3 * D_MODEL
A_GROUPS = A_HALF // 256
A_GROUP_WIDTH = A_HALF // A_GROUPS
POOL_WINDOWS = (2, 4, 8, 16)
POOL_GROUP = D_MODEL // len(POOL_WINDOWS)
CONV_WIDTH = 31
SHORT_CONV_WIDTH = 3
EPS = 1e-6

TOKENS = BATCH * SEQ
ROW_TILE = 512
TILES_PER_SEQ = SEQ // ROW_TILE
FF_CHUNK = 256
SUBLANES = 8
POOL_HALO = 16
CONV_HALO = 32
SHORT_HALO = 8
VMEM_LIMIT_BYTES = 56 * 1024 * 1024

BF16 = jnp.bfloat16
F32 = jnp.float32


def _dot(a, b):
    return jnp.dot(a, b, preferred_element_type=F32)


def _rms(x, g):
    return x * lax.rsqrt(jnp.mean(x * x, axis=-1, keepdims=True) + EPS) * g


def _layer_norm(x, g, b):
    mu = jnp.mean(x, axis=-1, keepdims=True)
    xc = x - mu
    return xc * lax.rsqrt(jnp.mean(xc * xc, axis=-1, keepdims=True) + EPS) * g + b


def _swiglu_tile(xn, wg_ref, wu_ref, wd_ref):
    acc = jnp.zeros((xn.shape[0], D_MODEL), F32)
    for c in range(D_FF // FF_CHUNK):
        cols = slice(c * FF_CHUNK, (c + 1) * FF_CHUNK)
        gate = _dot(xn, wg_ref[:, cols])
        up = _dot(xn, wu_ref[:, cols])
        hid = (gate * jax.nn.sigmoid(gate) * up).astype(BF16)
        acc = acc + _dot(hid, wd_ref[cols, :])
    return acc


def _ffn_kernel(x_ref, pre_g, wg, wu, wd, post_g, o_ref):
    x = x_ref[...]
    xn = _rms(x, pre_g[...]).astype(BF16)
    f = _swiglu_tile(xn, wg, wu, wd)
    o_ref[...] = x + 0.5 * _rms(f, post_g[...])


def _ffn_ple_kernel(x_ref, p_ref, pre_g, wg, wu, wd, post_g,
                    gn_g, w_gate, w_proj, ple_post_g, o_ref):
    x = x_ref[...]
    xn = _rms(x, pre_g[...]).astype(BF16)
    f = _swiglu_tile(xn, wg, wu, wd)
    h = x + 0.5 * _rms(f, post_g[...])
    gate = jax.nn.sigmoid(_dot(_rms(h, gn_g[...]).astype(BF16), w_gate[...]))
    e = _dot(p_ref[...].astype(BF16), w_proj[...]) * gate
    o_ref[...] = h + _rms(e, ple_post_g[...])


def _gmlp_kernel(x_ref, pre_g, w_in, v_g, v_b, w_s, b_s, w_out, post_g, o_ref, v_scr):
    x = x_ref[...]
    hn = _rms(x, pre_g[...]).astype(BF16)
    for g in range(A_GROUPS):
        cols = slice(A_HALF + g * A_GROUP_WIDTH, A_HALF + (g + 1) * A_GROUP_WIDTH)
        v_scr[:, g * A_GROUP_WIDTH:(g + 1) * A_GROUP_WIDTH] = jax.nn.gelu(_dot(hn, w_in[:, cols]))
    vn = _layer_norm(v_scr[...], v_g[...], v_b[...]).astype(BF16)
    row = lax.broadcasted_iota(jnp.int32, (CHUNK, CHUNK), 0)
    col = lax.broadcasted_iota(jnp.int32, (CHUNK, CHUNK), 1)
    causal = col <= row
    acc = jnp.zeros((ROW_TILE, D_MODEL), F32)
    for g in range(A_GROUPS):
        cols = slice(g * A_GROUP_WIDTH, (g + 1) * A_GROUP_WIDTH)
        u = jax.nn.gelu(_dot(hn, w_in[:, cols]))
        ws = jnp.where(causal, w_s[g], 0.0).astype(BF16)
        bias = b_s[g]
        sv = jnp.concatenate(
            [_dot(ws, vn[n * CHUNK:(n + 1) * CHUNK, cols]) + bias
             for n in range(ROW_TILE // CHUNK)], axis=0)
        acc = acc + _dot((u * sv).astype(BF16), w_out[cols, :])
    o_ref[...] = x + _rms(acc, post_g[...])


def _seq_tile_start():
    return pl.program_id(0) % TILES_PER_SEQ == 0


def _pool_kernel(x_ref, pre_g, w_grp, scale, post_g, o_ref, ext):
    x = x_ref[...]
    hn = _rms(x, pre_g[...])

    @pl.when(_seq_tile_start())
    def _():
        ext[0:POOL_HALO, :] = jnp.zeros((POOL_HALO, D_MODEL), F32)

    ext[POOL_HALO:POOL_HALO + ROW_TILE, :] = hn
    pos = (pl.program_id(0) % TILES_PER_SEQ) * ROW_TILE + lax.broadcasted_iota(
        jnp.int32, (ROW_TILE, 1), 0)
    outs = []
    for g, w in enumerate(POOL_WINDOWS):
        cols = slice(g * POOL_GROUP, (g + 1) * POOL_GROUP)
        total = ext[:, cols]
        span = 1
        while span < w:
            total = total + pltpu.roll(total, span, 0)
            span *= 2
        total = total[POOL_HALO:POOL_HALO + ROW_TILE, :]
        count = jnp.minimum(pos + 1, w).astype(F32)
        pooled = (total / count - hn[:, cols]).astype(BF16)
        outs.append(_dot(pooled, w_grp[g]))
    y = jnp.concatenate(outs, axis=-1) * scale[...]
    o_ref[...] = x + _rms(y, post_g[...])
    ext[0:POOL_HALO, :] = ext[ROW_TILE:ROW_TILE + POOL_HALO, :]


def _conformer_kernel(x_ref, pre_g, w_pw1, w_dw, b_dw, n_g, n_b, w_pw2, post_g, o_ref, ext):
    x = x_ref[...]
    hn = _rms(x, pre_g[...]).astype(BF16)
    a = _dot(hn, w_pw1[:, 0:D_MODEL])
    gate = _dot(hn, w_pw1[:, D_MODEL:2 * D_MODEL])

    @pl.when(_seq_tile_start())
    def _():
        ext[0:CONV_HALO, :] = jnp.zeros((CONV_HALO, D_MODEL), F32)

    ext[CONV_HALO:CONV_HALO + ROW_TILE, :] = a * jax.nn.sigmoid(gate)
    first = CONV_HALO - (CONV_WIDTH - 1)
    conv = jnp.zeros((ROW_TILE, D_MODEL), F32) + b_dw[...]
    ext_rows = CONV_HALO + ROW_TILE
    ext_val = ext[...]
    for r in range(SUBLANES):
        shifted = ext_val if r == 0 else pltpu.roll(ext_val, ext_rows - r, 0)
        for k in range(CONV_WIDTH):
            if (first + k) % SUBLANES == r:
                base = first + k - r
                conv = conv + shifted[base:base + ROW_TILE, :] * w_dw[k:k + 1, :]
    z = _layer_norm(conv, n_g[...], n_b[...])
    z = (z * jax.nn.sigmoid(z)).astype(BF16)
    y = _dot(z, w_pw2[...])
    o_ref[...] = x + _rms(y, post_g[...])
    ext[0:CONV_HALO, :] = ext[ROW_TILE:ROW_TILE + CONV_HALO, :]


def _short_conv_kernel(x_ref, pre_g, w_in, w_conv, w_out, post_g, o_ref, ext):
    x = x_ref[...]
    hn = _rms(x, pre_g[...]).astype(BF16)
    bg = _dot(hn, w_in[:, 0:D_MODEL])
    cg = _dot(hn, w_in[:, D_MODEL:2 * D_MODEL])
    xv = _dot(hn, w_in[:, 2 * D_MODEL:3 * D_MODEL])

    @pl.when(_seq_tile_start())
    def _():
        ext[0:SHORT_HALO, :] = jnp.zeros((SHORT_HALO, D_MODEL), F32)

    ext[SHORT_HALO:SHORT_HALO + ROW_TILE, :] = cg * xv
    first = SHORT_HALO - (SHORT_CONV_WIDTH - 1)
    conv = jnp.zeros((ROW_TILE, D_MODEL), F32)
    for k in range(SHORT_CONV_WIDTH):
        conv = conv + ext[first + k:first + k + ROW_TILE, :] * w_conv[k:k + 1, :]
    y = _dot((bg * conv).astype(BF16), w_out[...])
    o_ref[...] = x + _rms(y, post_g[...])
    ext[0:SHORT_HALO, :] = ext[ROW_TILE:ROW_TILE + SHORT_HALO, :]


def _row_spec(width):
    return pl.BlockSpec((ROW_TILE, width), lambda s: (s, 0))


def _stacked_row_spec(width, layer):
    return pl.BlockSpec((None, ROW_TILE, width), lambda s: (layer, s, 0))


def _resident_spec(stacked, layer):
    zeros = (0,) * (stacked.ndim - 1)
    return pl.BlockSpec((None,) + stacked.shape[1:], lambda s: (layer,) + zeros,
                        pipeline_mode=pl.Buffered(1))


def _call(body, name, row_inputs, row_specs, resident_inputs, scratch_shapes=()):
    in_specs = list(row_specs) + [_resident_spec(a, layer) for a, layer in resident_inputs]
    return pl.pallas_call(
        body,
        name=name,
        grid=(TOKENS // ROW_TILE,),
        in_specs=in_specs,
        out_specs=_row_spec(D_MODEL),
        out_shape=jax.ShapeDtypeStruct((TOKENS, D_MODEL), F32),
        scratch_shapes=list(scratch_shapes),
        compiler_params=pltpu.CompilerParams(
            dimension_semantics=("arbitrary",),
            vmem_limit_bytes=VMEM_LIMIT_BYTES),
    )(*row_inputs, *[a for a, _ in resident_inputs])


def _vecs(v):
    return v.reshape(v.shape[0], 1, v.shape[1])


def kernel(x, p, ff1_pre_g, ff1_w_gate, ff1_w_up, ff1_w_down, ff1_post_g, mix_pre_g, mix_post_g, ff2_pre_g, ff2_w_gate, ff2_w_up, ff2_w_down, ff2_post_g, ple_gate_norm_g, ple_w_gate, ple_w_proj, ple_post_g, a_w_in, a_v_norm_g, a_v_norm_b, a_w_s, a_b_s, a_w_out, b_w_grp, b_scale, c_w_pw1, c_w_dw, c_b_dw, c_norm_g, c_norm_b, c_w_pw2, d_w_in, d_w_conv, d_w_out):
    bf = lambda w: w.astype(BF16)
    ff1_w_gate, ff1_w_up, ff1_w_down = bf(ff1_w_gate), bf(ff1_w_up), bf(ff1_w_down)
    ff2_w_gate, ff2_w_up, ff2_w_down = bf(ff2_w_gate), bf(ff2_w_up), bf(ff2_w_down)
    ple_w_gate, ple_w_proj = bf(ple_w_gate), bf(ple_w_proj)
    a_w_in, a_w_out, b_w_grp = bf(a_w_in), bf(a_w_out), bf(b_w_grp)
    c_w_pw1, c_w_pw2, d_w_in, d_w_out = bf(c_w_pw1), bf(c_w_pw2), bf(d_w_in), bf(d_w_out)

    ff1_pre_g, ff1_post_g, ff2_pre_g, ff2_post_g = map(
        _vecs, (ff1_pre_g, ff1_post_g, ff2_pre_g, ff2_post_g))
    mix_pre_g, mix_post_g = _vecs(mix_pre_g), _vecs(mix_post_g)
    ple_gate_norm_g, ple_post_g = _vecs(ple_gate_norm_g), _vecs(ple_post_g)
    a_v_norm_g, a_v_norm_b, b_scale = _vecs(a_v_norm_g), _vecs(a_v_norm_b), _vecs(b_scale)
    c_b_dw, c_norm_g, c_norm_b = _vecs(c_b_dw), _vecs(c_norm_g), _vecs(c_norm_b)
    a_b_s = a_b_s.reshape(a_b_s.shape[0], A_GROUPS, CHUNK, 1)

    h = x.reshape(TOKENS, D_MODEL)
    p = p.reshape(DEPTH, TOKENS, PLE_DIM)
    h_spec = _row_spec(D_MODEL)
    for i in range(DEPTH):
        h = _call(_ffn_kernel, f"ffn1_l{i}", [h], [h_spec],
                  [(ff1_pre_g, i), (ff1_w_gate, i), (ff1_w_up, i), (ff1_w_down, i),
                   (ff1_post_g, i)])
        m, j = i % N_MIXERS, i // N_MIXERS
        pre, post = (mix_pre_g, i), (mix_post_g, i)
        if m == 0:
            h = _call(_gmlp_kernel, f"gmlp_l{i}", [h], [h_spec],
                      [pre, (a_w_in, j), (a_v_norm_g, j), (a_v_norm_b, j), (a_w_s, j),
                       (a_b_s, j), (a_w_out, j), post],
                      [pltpu.VMEM((ROW_TILE, A_HALF), F32)])
        elif m == 1:
            h = _call(_pool_kernel, f"pool_l{i}", [h], [h_spec],
                      [pre, (b_w_grp, j), (b_scale, j), post],
                      [pltpu.VMEM((POOL_HALO + ROW_TILE, D_MODEL), F32)])
        elif m == 2:
            h = _call(_conformer_kernel, f"conformer_l{i}", [h], [h_spec],
                      [pre, (c_w_pw1, j), (c_w_dw, j), (c_b_dw, j), (c_norm_g, j),
                       (c_norm_b, j), (c_w_pw2, j), post],
                      [pltpu.VMEM((CONV_HALO + ROW_TILE, D_MODEL), F32)])
        else:
            h = _call(_short_conv_kernel, f"shortconv_l{i}", [h], [h_spec],
                      [pre, (d_w_in, j), (d_w_conv, j), (d_w_out, j), post],
                      [pltpu.VMEM((SHORT_HALO + ROW_TILE, D_MODEL), F32)])
        h = _call(_ffn_ple_kernel, f"ffn2_ple_l{i}", [h, p],
                  [h_spec, _stacked_row_spec(PLE_DIM, i)],
                  [(ff2_pre_g, i), (ff2_w_gate, i), (ff2_w_up, i), (ff2_w_down, i),
                   (ff2_post_g, i), (ple_gate_norm_g, i), (ple_w_gate, i), (ple_w_proj, i),
                   (ple_post_g, i)])
    return h.reshape(BATCH, SEQ, D_MODEL)
```

```python
import collections
import functools

import jax
import jax.numpy as jnp
from jax import lax
from jax.experimental import pallas as pl
from jax.experimental.pallas import tpu as pltpu

D_MODEL = 1024
BATCH = 8
SEQ = 4096
DEPTH = 4
N_MIXERS = 4
D_FF = 2816
PLE_DIM = 256
CHUNK = 128
A_HALF = 3 * D_MODEL
A_GROUPS = A_HALF // 256
A_GROUP_WIDTH = A_HALF // A_GROUPS
POOL_WINDOWS = (2, 4, 8, 16)
POOL_GROUP = D_MODEL // len(POOL_WINDOWS)
CONV_WIDTH = 31
SHORT_CONV_WIDTH = 3
EPS = 1e-6

TOKENS = BATCH * SEQ
ROW_TILE = 512
N_TILES = TOKENS // ROW_TILE
TILES_PER_SEQ = SEQ // ROW_TILE
FF_CHUNK = 256
SUBLANES = 8
LANES = 128
POOL_HALO = 16
CONV_HALO = 32
SHORT_HALO = 8
VMEM_LIMIT_BYTES = 56 * 1024 * 1024

BF16 = jnp.bfloat16
F32 = jnp.float32


def _dot(a, b):
    return jnp.dot(a, b, preferred_element_type=F32)


def _rms(x, g):
    return x * lax.rsqrt(jnp.mean(x * x, axis=-1, keepdims=True) + EPS) * g


def _layer_norm(x, g, b):
    mu = jnp.mean(x, axis=-1, keepdims=True)
    xc = x - mu
    return xc * lax.rsqrt(jnp.mean(xc * xc, axis=-1, keepdims=True) + EPS) * g + b


def _swiglu_tile(xn, wg_ref, wu_ref, wd_ref):
    acc = jnp.zeros((xn.shape[0], D_MODEL), F32)
    for c in range(D_FF // FF_CHUNK):
        cols = slice(c * FF_CHUNK, (c + 1) * FF_CHUNK)
        gate = _dot(xn, wg_ref[:, cols])
        up = _dot(xn, wu_ref[:, cols])
        hid = (gate * jax.nn.sigmoid(gate) * up).astype(BF16)
        acc = acc + _dot(hid, wd_ref[cols, :])
    return acc


def _ffn_kernel(x_ref, pre_g, wg, wu, wd, post_g, o_ref):
    x = x_ref[...]
    xn = _rms(x, pre_g[...]).astype(BF16)
    f = _swiglu_tile(xn, wg, wu, wd)
    o_ref[...] = x + 0.5 * _rms(f, post_g[...])


def _gmlp_kernel(x_ref, pre_g, w_in, v_g, v_b, w_s, b_s, w_out, post_g, o_ref, v_scr):
    x = x_ref[...]
    hn = _rms(x, pre_g[...]).astype(BF16)
    for g in range(A_GROUPS):
        cols = slice(A_HALF + g * A_GROUP_WIDTH, A_HALF + (g + 1) * A_GROUP_WIDTH)
        v_scr[:, g * A_GROUP_WIDTH:(g + 1) * A_GROUP_WIDTH] = jax.nn.gelu(_dot(hn, w_in[:, cols]))
    vn = _layer_norm(v_scr[...], v_g[...], v_b[...]).astype(BF16)
    row = lax.broadcasted_iota(jnp.int32, (CHUNK, CHUNK), 0)
    col = lax.broadcasted_iota(jnp.int32, (CHUNK, CHUNK), 1)
    causal = col <= row
    acc = jnp.zeros((ROW_TILE, D_MODEL), F32)
    for g in range(A_GROUPS):
        cols = slice(g * A_GROUP_WIDTH, (g + 1) * A_GROUP_WIDTH)
        u = jax.nn.gelu(_dot(hn, w_in[:, cols]))
        ws = jnp.where(causal, w_s[g], 0.0).astype(BF16)
        bias = b_s[g]
        sv = jnp.concatenate(
            [_dot(ws, vn[n * CHUNK:(n + 1) * CHUNK, cols]) + bias
             for n in range(ROW_TILE // CHUNK)], axis=0)
        acc = acc + _dot((u * sv).astype(BF16), w_out[cols, :])
    o_ref[...] = x + _rms(acc, post_g[...])


Lag = collections.namedtuple("Lag", ["tile_in_seq", "next_is_seq_start"])


def _zero_after(value):
    bits = pltpu.bitcast(value[0:SUBLANES, 0:LANES], jnp.uint32)
    bits = lax.shift_right_logical(lax.shift_right_logical(bits, jnp.uint32(16)), jnp.uint32(16))
    return pltpu.bitcast(bits, F32)[0:1, 0:1]


def _after(value, pace):
    if not pace:
        return value
    return value + _zero_after(pace[-1]).astype(value.dtype)


def _interleave(*gens):
    live = list(gens)
    while live:
        for g in list(live):
            try:
                next(g)
            except StopIteration:
                live.remove(g)


def _ffn_steps(x_ref, pre_g, wg, wu, wd, post_g, ffn_pace, other_pace, result):
    xn = _rms(x_ref[...], pre_g[...]).astype(BF16)
    acc = jnp.zeros((ROW_TILE, D_MODEL), F32)
    yield
    for c in range(D_FF // FF_CHUNK):
        cols = slice(c * FF_CHUNK, (c + 1) * FF_CHUNK)
        xc = _after(xn, other_pace)
        gate = _dot(xc, wg[:, cols])
        up = _dot(xc, wu[:, cols])
        hid = (gate * jax.nn.sigmoid(gate) * up).astype(BF16)
        acc = acc + _dot(hid, wd[cols, :])
        ffn_pace.append(acc)
        yield
    result.append(x_ref[...] + 0.5 * _rms(acc, post_g[...]))


def _causal_taps(ext_val, halo, w_ref, cols, init):
    n_taps = w_ref.shape[0]
    first = halo - (n_taps - 1)
    ext_rows = halo + ROW_TILE
    out = init
    for r in range(SUBLANES):
        taps = [k for k in range(n_taps) if (first + k) % SUBLANES == r]
        if not taps:
            continue
        shifted = ext_val if r == 0 else pltpu.roll(ext_val, ext_rows - r, 0)
        for k in taps:
            base = first + k - r
            out = out + shifted[base:base + ROW_TILE, :] * w_ref[k:k + 1, cols]
    return out


def _pool_steps(h_ref, lag, pre_g, w_grp, scale, post_g, ext, ffn_pace, own_pace, result):
    ext[POOL_HALO:POOL_HALO + ROW_TILE, :] = _rms(h_ref[...], pre_g[...])
    yield
    pos = lag.tile_in_seq * ROW_TILE + lax.broadcasted_iota(jnp.int32, (ROW_TILE, 1), 0)
    outs = []
    for g, w in enumerate(POOL_WINDOWS):
        cols = slice(g * POOL_GROUP, (g + 1) * POOL_GROUP)
        total = _after(ext[:, cols], ffn_pace)
        span = 1
        while span < w:
            total = total + pltpu.roll(total, span, 0)
            span *= 2
        total = total[POOL_HALO:POOL_HALO + ROW_TILE, :]
        count = jnp.minimum(pos + 1, w).astype(F32)
        pooled = (total / count - ext[POOL_HALO:POOL_HALO + ROW_TILE, cols]).astype(BF16)
        outs.append(_dot(pooled, w_grp[g]))
        own_pace.append(outs[-1])
        yield
    y = jnp.concatenate(outs, axis=-1) * scale[...]
    tail = ext[ROW_TILE:ROW_TILE + POOL_HALO, :]
    ext[0:POOL_HALO, :] = jnp.where(lag.next_is_seq_start, 0.0, tail)
    result.append(h_ref[...] + _rms(y, post_g[...]))


def _conformer_steps(h_ref, lag, pre_g, w_pw1, w_dw, b_dw, n_g, n_b, w_pw2, post_g, ext,
                     ffn_pace, own_pace, result):
    hn = _rms(h_ref[...], pre_g[...]).astype(BF16)
    a = _dot(hn, w_pw1[:, 0:D_MODEL])
    gate = _dot(hn, w_pw1[:, D_MODEL:2 * D_MODEL])
    ext[CONV_HALO:CONV_HALO + ROW_TILE, :] = a * jax.nn.sigmoid(gate)
    yield
    conv_cols = []
    for cb in range(D_MODEL // LANES):
        cols = slice(cb * LANES, (cb + 1) * LANES)
        bias = jnp.zeros((ROW_TILE, LANES), F32) + b_dw[:, cols]
        conv = _causal_taps(_after(ext[:, cols], ffn_pace), CONV_HALO, w_dw, cols, bias)
        conv_cols.append(conv)
        own_pace.append(conv)
        yield
    z = _layer_norm(_after(jnp.concatenate(conv_cols, axis=-1), ffn_pace), n_g[...], n_b[...])
    z = (z * jax.nn.sigmoid(z)).astype(BF16)
    yield
    y = _dot(z, w_pw2[...])
    yield
    tail = ext[ROW_TILE:ROW_TILE + CONV_HALO, :]
    ext[0:CONV_HALO, :] = jnp.where(lag.next_is_seq_start, 0.0, tail)
    result.append(h_ref[...] + _rms(y, post_g[...]))


def _short_conv_steps(h_ref, lag, pre_g, w_in, w_conv, w_out, post_g, ext,
                      ffn_pace, own_pace, result):
    hn = _rms(h_ref[...], pre_g[...]).astype(BF16)
    cg = _dot(hn, w_in[:, D_MODEL:2 * D_MODEL])
    xv = _dot(hn, w_in[:, 2 * D_MODEL:3 * D_MODEL])
    ext[SHORT_HALO:SHORT_HALO + ROW_TILE, :] = cg * xv
    yield
    bg = _dot(_after(hn, ffn_pace), w_in[:, 0:D_MODEL])
    own_pace.append(bg)
    yield
    gated_cols = []
    for cb in range(D_MODEL // LANES):
        cols = slice(cb * LANES, (cb + 1) * LANES)
        conv = _causal_taps(_after(ext[:, cols], ffn_pace), SHORT_HALO, w_conv, cols,
                            jnp.zeros((ROW_TILE, LANES), F32))
        gated_cols.append((bg[:, cols] * conv).astype(BF16))
        own_pace.append(conv)
        yield
    y = _dot(jnp.concatenate(gated_cols, axis=-1), w_out[...])
    yield
    tail = ext[ROW_TILE:ROW_TILE + SHORT_HALO, :]
    ext[0:SHORT_HALO, :] = jnp.where(lag.next_is_seq_start, 0.0, tail)
    result.append(h_ref[...] + _rms(y, post_g[...]))


def _ple_steps(h_ref, lag, p_ref, gn_g, w_gate, w_proj, post_g, ffn_pace, own_pace, result):
    gate_in = _rms(h_ref[...], gn_g[...]).astype(BF16)
    gate_lin = _dot(gate_in, w_gate[...])
    yield
    proj = _dot(p_ref[...].astype(BF16), w_proj[...])
    e = proj * jax.nn.sigmoid(_after(gate_lin, ffn_pace))
    own_pace.append(e)
    yield
    result.append(h_ref[...] + _rms(_after(e, ffn_pace), post_g[...]))


N_FFN_RESIDENT = 5


def _ffn_then_stage_kernel(stage_steps, n_stage_rows, n_stage_resident, x_ref, *refs):
    stage_rows, refs = refs[:n_stage_rows], refs[n_stage_rows:]
    ffn_res, refs = refs[:N_FFN_RESIDENT], refs[N_FFN_RESIDENT:]
    stage_res, refs = refs[:n_stage_resident], refs[n_stage_resident:]
    o_ref, hbuf, scratch = refs[0], refs[1], refs[2:]
    s = pl.program_id(0)

    @pl.when(s == 0)
    def _():
        hbuf[...] = jnp.zeros(hbuf.shape, F32)
        for scr in scratch:
            scr[...] = jnp.zeros(scr.shape, scr.dtype)

    lag = Lag(tile_in_seq=(s + TILES_PER_SEQ - 1) % TILES_PER_SEQ,
              next_is_seq_start=s % TILES_PER_SEQ == 0)
    ffn_pace, stage_pace, ffn_out, stage_out = [], [], [], []
    _interleave(
        _ffn_steps(x_ref, *ffn_res, ffn_pace, stage_pace, ffn_out),
        stage_steps(hbuf, lag, *stage_rows, *stage_res, *scratch, ffn_pace, stage_pace,
                    stage_out))
    o_ref[...] = stage_out[0]
    hbuf[...] = ffn_out[0]


def _row_spec(width):
    return pl.BlockSpec((ROW_TILE, width), lambda s: (s, 0))


def _resident_spec(stacked, layer):
    zeros = (0,) * (stacked.ndim - 1)
    return pl.BlockSpec((None,) + stacked.shape[1:], lambda s: (layer,) + zeros,
                        pipeline_mode=pl.Buffered(1))


def _compiler_params():
    return pltpu.CompilerParams(dimension_semantics=("arbitrary",),
                                vmem_limit_bytes=VMEM_LIMIT_BYTES)


def _call(body, name, x, resident_inputs, scratch_shapes=()):
    return pl.pallas_call(
        body,
        name=name,
        grid=(N_TILES,),
        in_specs=[_row_spec(D_MODEL)] + [_resident_spec(a, l) for a, l in resident_inputs],
        out_specs=_row_spec(D_MODEL),
        out_shape=jax.ShapeDtypeStruct((TOKENS, D_MODEL), F32),
        scratch_shapes=list(scratch_shapes),
        compiler_params=_compiler_params(),
    )(x, *[a for a, _ in resident_inputs])


def _fused_call(stage_steps, name, x, ffn_resident, stage_resident, stage_rows=(),
                scratch_shapes=()):
    resident = list(ffn_resident) + list(stage_resident)
    in_specs = [pl.BlockSpec((ROW_TILE, D_MODEL), lambda s: (jnp.minimum(s, N_TILES - 1), 0))]
    for a, layer in stage_rows:
        in_specs.append(pl.BlockSpec((None, ROW_TILE, a.shape[-1]),
                                     lambda s, layer=layer: (layer, jnp.maximum(s - 1, 0), 0)))
    in_specs += [_resident_spec(a, l) for a, l in resident]
    return pl.pallas_call(
        functools.partial(_ffn_then_stage_kernel, stage_steps, len(stage_rows),
                          len(stage_resident)),
        name=name,
        grid=(N_TILES + 1,),
        in_specs=in_specs,
        out_specs=pl.BlockSpec((ROW_TILE, D_MODEL), lambda s: (jnp.maximum(s - 1, 0), 0)),
        out_shape=jax.ShapeDtypeStruct((TOKENS, D_MODEL), F32),
        scratch_shapes=[pltpu.VMEM((ROW_TILE, D_MODEL), F32)] + list(scratch_shapes),
        compiler_params=_compiler_params(),
    )(x, *[a for a, _ in stage_rows], *[a for a, _ in resident])


def _vecs(v):
    return v.reshape(v.shape[0], 1, v.shape[1])


def kernel(x, p, ff1_pre_g, ff1_w_gate, ff1_w_up, ff1_w_down, ff1_post_g, mix_pre_g, mix_post_g, ff2_pre_g, ff2_w_gate, ff2_w_up, ff2_w_down, ff2_post_g, ple_gate_norm_g, ple_w_gate, ple_w_proj, ple_post_g, a_w_in, a_v_norm_g, a_v_norm_b, a_w_s, a_b_s, a_w_out, b_w_grp, b_scale, c_w_pw1, c_w_dw, c_b_dw, c_norm_g, c_norm_b, c_w_pw2, d_w_in, d_w_conv, d_w_out):
    bf = lambda w: w.astype(BF16)
    ff1_w_gate, ff1_w_up, ff1_w_down = bf(ff1_w_gate), bf(ff1_w_up), bf(ff1_w_down)
    ff2_w_gate, ff2_w_up, ff2_w_down = bf(ff2_w_gate), bf(ff2_w_up), bf(ff2_w_down)
    ple_w_gate, ple_w_proj = bf(ple_w_gate), bf(ple_w_proj)
    a_w_in, a_w_out, b_w_grp = bf(a_w_in), bf(a_w_out), bf(b_w_grp)
    c_w_pw1, c_w_pw2, d_w_in, d_w_out = bf(c_w_pw1), bf(c_w_pw2), bf(d_w_in), bf(d_w_out)

    ff1_pre_g, ff1_post_g, ff2_pre_g, ff2_post_g = map(
        _vecs, (ff1_pre_g, ff1_post_g, ff2_pre_g, ff2_post_g))
    mix_pre_g, mix_post_g = _vecs(mix_pre_g), _vecs(mix_post_g)
    ple_gate_norm_g, ple_post_g = _vecs(ple_gate_norm_g), _vecs(ple_post_g)
    a_v_norm_g, a_v_norm_b, b_scale = _vecs(a_v_norm_g), _vecs(a_v_norm_b), _vecs(b_scale)
    c_b_dw, c_norm_g, c_norm_b = _vecs(c_b_dw), _vecs(c_norm_g), _vecs(c_norm_b)
    a_b_s = a_b_s.reshape(a_b_s.shape[0], A_GROUPS, CHUNK, 1)

    h = x.reshape(TOKENS, D_MODEL)
    p = p.reshape(DEPTH, TOKENS, PLE_DIM)
    for i in range(DEPTH):
        ffn1 = [(ff1_pre_g, i), (ff1_w_gate, i), (ff1_w_up, i), (ff1_w_down, i), (ff1_post_g, i)]
        m, j = i % N_MIXERS, i // N_MIXERS
        pre, post = (mix_pre_g, i), (mix_post_g, i)
        if m == 0:
            h = _call(_ffn_kernel, f"ffn1_l{i}", h, ffn1)
            h = _call(_gmlp_kernel, f"gmlp_l{i}", h,
                      [pre, (a_w_in, j), (a_v_norm_g, j), (a_v_norm_b, j), (a_w_s, j),
                       (a_b_s, j), (a_w_out, j), post],
                      [pltpu.VMEM((ROW_TILE, A_HALF), F32)])
        elif m == 1:
            h = _fused_call(_pool_steps, f"ffn1_pool_l{i}", h, ffn1,
                            [pre, (b_w_grp, j), (b_scale, j), post],
                            scratch_shapes=[pltpu.VMEM((POOL_HALO + ROW_TILE, D_MODEL), F32)])
        elif m == 2:
            h = _fused_call(_conformer_steps, f"ffn1_conformer_l{i}", h, ffn1,
                            [pre, (c_w_pw1, j), (c_w_dw, j), (c_b_dw, j), (c_norm_g, j),
                             (c_norm_b, j), (c_w_pw2, j), post],
                            scratch_shapes=[pltpu.VMEM((CONV_HALO + ROW_TILE, D_MODEL), F32)])
        else:
            h = _fused_call(_short_conv_steps, f"ffn1_shortconv_l{i}", h, ffn1,
                            [pre, (d_w_in, j), (d_w_conv, j), (d_w_out, j), post],
                            scratch_shapes=[pltpu.VMEM((SHORT_HALO + ROW_TILE, D_MODEL), F32)])
        h = _fused_call(_ple_steps, f"ffn2_ple_l{i}", h,
                        [(ff2_pre_g, i), (ff2_w_gate, i), (ff2_w_up, i), (ff2_w_down, i),
                         (ff2_post_g, i)],
                        [(ple_gate_norm_g, i), (ple_w_gate, i), (ple_w_proj, i), (ple_post_g, i)],
                        stage_rows=[(p, i)])
    return h.reshape(BATCH, SEQ, D_MODEL)
```

```python
import collections
import functools

import jax
import jax.numpy as jnp
from jax import lax
from jax.experimental import pallas as pl
from jax.experimental.pallas import tpu as pltpu

D_MODEL = 1024
BATCH = 8
SEQ = 4096
DEPTH = 4
N_MIXERS = 4
D_FF = 2816
PLE_DIM = 256
CHUNK = 128
A_HALF = 3 * D_MODEL
A_GROUPS = A_HALF // 256
A_GROUP_WIDTH = A_HALF // A_GROUPS
POOL_WINDOWS = (2, 4, 8, 16)
POOL_GROUP = D_MODEL // len(POOL_WINDOWS)
CONV_WIDTH = 31
SHORT_CONV_WIDTH = 3
EPS = 1e-6

TOKENS = BATCH * SEQ
ROW_TILE = 512
N_TILES = TOKENS // ROW_TILE
TILES_PER_SEQ = SEQ // ROW_TILE
FF_CHUNK = 256
SUBLANES = 8
LANES = 128
POOL_HALO = 16
CONV_HALO = 32
SHORT_HALO = 8
VMEM_LIMIT_BYTES = 56 * 1024 * 1024

BF16 = jnp.bfloat16
F32 = jnp.float32


def _dot(a, b):
    return jnp.dot(a, b, preferred_element_type=F32)


def _rms(x, g):
    return x * lax.rsqrt(jnp.mean(x * x, axis=-1, keepdims=True) + EPS) * g


def _layer_norm(x, g, b):
    mu = jnp.mean(x, axis=-1, keepdims=True)
    xc = x - mu
    return xc * lax.rsqrt(jnp.mean(xc * xc, axis=-1, keepdims=True) + EPS) * g + b


def _swiglu_tile(xn, wg_ref, wu_ref, wd_ref):
    acc = jnp.zeros((xn.shape[0], D_MODEL), F32)
    for c in range(D_FF // FF_CHUNK):
        cols = slice(c * FF_CHUNK, (c + 1) * FF_CHUNK)
        gate = _dot(xn, wg_ref[:, cols])
        up = _dot(xn, wu_ref[:, cols])
        hid = (gate * jax.nn.sigmoid(gate) * up).astype(BF16)
        acc = acc + _dot(hid, wd_ref[cols, :])
    return acc


def _ffn_kernel(x_ref, pre_g, wg, wu, wd, post_g, o_ref):
    x = x_ref[...]
    xn = _rms(x, pre_g[...]).astype(BF16)
    f = _swiglu_tile(xn, wg, wu, wd)
    o_ref[...] = x + 0.5 * _rms(f, post_g[...])


def _gmlp_kernel(x_ref, pre_g, w_in, v_g, v_b, w_s, b_s, w_out, post_g, o_ref, z_scr):
    hn = _rms(x_ref[...], pre_g[...]).astype(BF16)
    for g in range(2 * A_GROUPS):
        cols = slice(g * A_GROUP_WIDTH, (g + 1) * A_GROUP_WIDTH)
        z_scr[:, cols] = jax.nn.gelu(_dot(hn, w_in[:, cols]))
    vn = _layer_norm(z_scr[:, A_HALF:2 * A_HALF], v_g[...], v_b[...]).astype(BF16)
    row = lax.broadcasted_iota(jnp.int32, (CHUNK, CHUNK), 0)
    col = lax.broadcasted_iota(jnp.int32, (CHUNK, CHUNK), 1)
    causal = col <= row
    acc = jnp.zeros((ROW_TILE, D_MODEL), F32)
    for g in range(A_GROUPS):
        cols = slice(g * A_GROUP_WIDTH, (g + 1) * A_GROUP_WIDTH)
        ws = jnp.where(causal, w_s[g], 0.0).astype(BF16)
        bias = b_s[g]
        sv = jnp.concatenate(
            [_dot(ws, vn[n * CHUNK:(n + 1) * CHUNK, cols]) + bias
             for n in range(ROW_TILE // CHUNK)], axis=0)
        acc = acc + _dot((z_scr[:, cols] * sv).astype(BF16), w_out[cols, :])
    o_ref[...] = x_ref[...] + _rms(acc, post_g[...])


Lag = collections.namedtuple("Lag", ["tile_in_seq", "next_is_seq_start"])


def _zero_after(value):
    bits = pltpu.bitcast(value[0:SUBLANES, 0:LANES], jnp.uint32)
    bits = lax.shift_right_logical(lax.shift_right_logical(bits, jnp.uint32(16)), jnp.uint32(16))
    return pltpu.bitcast(bits, F32)[0:1, 0:1]


def _after(value, pace):
    if not pace:
        return value
    return value + _zero_after(pace[-1]).astype(value.dtype)


def _interleave(*gens):
    live = list(gens)
    while live:
        for g in list(live):
            try:
                next(g)
            except StopIteration:
                live.remove(g)


def _ffn_steps(x_ref, pre_g, wg, wu, wd, ffn_pace, other_pace, result):
    xn = _rms(x_ref[...], pre_g[...]).astype(BF16)
    acc = jnp.zeros((ROW_TILE, D_MODEL), F32)
    yield
    for c in range(D_FF // FF_CHUNK):
        cols = slice(c * FF_CHUNK, (c + 1) * FF_CHUNK)
        xc = _after(xn, other_pace)
        gate = _dot(xc, wg[:, cols])
        up = _dot(xc, wu[:, cols])
        hid = (gate * jax.nn.sigmoid(gate) * up).astype(BF16)
        acc = acc + _dot(hid, wd[cols, :])
        ffn_pace.append(acc)
        yield
    result.append(acc)


def _ffn_finish(x_ref, acc_ref, post_g):
    return x_ref[...] + 0.5 * _rms(acc_ref[...], post_g[...])


def _causal_taps(ext_val, halo, w_ref, cols, init):
    n_taps = w_ref.shape[0]
    first = halo - (n_taps - 1)
    ext_rows = halo + ROW_TILE
    out = init
    for r in range(SUBLANES):
        taps = [k for k in range(n_taps) if (first + k) % SUBLANES == r]
        if not taps:
            continue
        shifted = ext_val if r == 0 else pltpu.roll(ext_val, ext_rows - r, 0)
        for k in taps:
            base = first + k - r
            out = out + shifted[base:base + ROW_TILE, :] * w_ref[k:k + 1, cols]
    return out


def _pool_steps(h_ref, lag, pre_g, w_grp, scale, post_g, ext, ffn_pace, own_pace, result):
    ext[POOL_HALO:POOL_HALO + ROW_TILE, :] = _rms(h_ref[...], pre_g[...])
    yield
    pos = lag.tile_in_seq * ROW_TILE + lax.broadcasted_iota(jnp.int32, (ROW_TILE, 1), 0)
    outs = []
    for g, w in enumerate(POOL_WINDOWS):
        cols = slice(g * POOL_GROUP, (g + 1) * POOL_GROUP)
        total = _after(ext[:, cols], ffn_pace)
        span = 1
        while span < w:
            total = total + pltpu.roll(total, span, 0)
            span *= 2
        total = total[POOL_HALO:POOL_HALO + ROW_TILE, :]
        count = jnp.minimum(pos + 1, w).astype(F32)
        pooled = (total / count - ext[POOL_HALO:POOL_HALO + ROW_TILE, cols]).astype(BF16)
        outs.append(_dot(pooled, w_grp[g]))
        own_pace.append(outs[-1])
        yield
    y = jnp.concatenate(outs, axis=-1) * scale[...]
    tail = ext[ROW_TILE:ROW_TILE + POOL_HALO, :]
    ext[0:POOL_HALO, :] = jnp.where(lag.next_is_seq_start, 0.0, tail)
    result.append(h_ref[...] + _rms(y, post_g[...]))


def _conformer_steps(h_ref, lag, pre_g, w_pw1, w_dw, b_dw, n_g, n_b, w_pw2, post_g, ext,
                     ffn_pace, own_pace, result):
    hn = _rms(h_ref[...], pre_g[...]).astype(BF16)
    a = _dot(hn, w_pw1[:, 0:D_MODEL])
    gate = _dot(hn, w_pw1[:, D_MODEL:2 * D_MODEL])
    ext[CONV_HALO:CONV_HALO + ROW_TILE, :] = a * jax.nn.sigmoid(gate)
    yield
    conv_cols = []
    for cb in range(D_MODEL // LANES):
        cols = slice(cb * LANES, (cb + 1) * LANES)
        bias = jnp.zeros((ROW_TILE, LANES), F32) + b_dw[:, cols]
        conv = _causal_taps(_after(ext[:, cols], ffn_pace), CONV_HALO, w_dw, cols, bias)
        conv_cols.append(conv)
        own_pace.append(conv)
        yield
    z = _layer_norm(_after(jnp.concatenate(conv_cols, axis=-1), ffn_pace), n_g[...], n_b[...])
    z = (z * jax.nn.sigmoid(z)).astype(BF16)
    yield
    y = _dot(z, w_pw2[...])
    yield
    tail = ext[ROW_TILE:ROW_TILE + CONV_HALO, :]
    ext[0:CONV_HALO, :] = jnp.where(lag.next_is_seq_start, 0.0, tail)
    result.append(h_ref[...] + _rms(y, post_g[...]))


def _short_conv_steps(h_ref, lag, pre_g, w_in, w_conv, w_out, post_g, ext,
                      ffn_pace, own_pace, result):
    hn = _rms(h_ref[...], pre_g[...]).astype(BF16)
    cg = _dot(hn, w_in[:, D_MODEL:2 * D_MODEL])
    xv = _dot(hn, w_in[:, 2 * D_MODEL:3 * D_MODEL])
    ext[SHORT_HALO:SHORT_HALO + ROW_TILE, :] = cg * xv
    yield
    bg = _dot(_after(hn, ffn_pace), w_in[:, 0:D_MODEL])
    own_pace.append(bg)
    yield
    gated_cols = []
    for cb in range(D_MODEL // LANES):
        cols = slice(cb * LANES, (cb + 1) * LANES)
        conv = _causal_taps(_after(ext[:, cols], ffn_pace), SHORT_HALO, w_conv, cols,
                            jnp.zeros((ROW_TILE, LANES), F32))
        gated_cols.append((bg[:, cols] * conv).astype(BF16))
        own_pace.append(conv)
        yield
    y = _dot(jnp.concatenate(gated_cols, axis=-1), w_out[...])
    yield
    tail = ext[ROW_TILE:ROW_TILE + SHORT_HALO, :]
    ext[0:SHORT_HALO, :] = jnp.where(lag.next_is_seq_start, 0.0, tail)
    result.append(h_ref[...] + _rms(y, post_g[...]))


def _ple_steps(h_ref, lag, p_ref, gn_g, w_gate, w_proj, post_g, ffn_pace, own_pace, result):
    gate_in = _rms(h_ref[...], gn_g[...]).astype(BF16)
    gate_lin = _dot(gate_in, w_gate[...])
    yield
    proj = _dot(p_ref[...].astype(BF16), w_proj[...])
    e = proj * jax.nn.sigmoid(_after(gate_lin, ffn_pace))
    own_pace.append(e)
    yield
    result.append(h_ref[...] + _rms(_after(e, ffn_pace), post_g[...]))


N_FFN_RESIDENT = 5


def _ffn_then_stage_kernel(stage_steps, n_stage_rows, n_stage_resident, x_ref, x_prev_ref,
                           *refs):
    stage_rows, refs = refs[:n_stage_rows], refs[n_stage_rows:]
    (pre_g, wg, wu, wd, post_g), refs = refs[:N_FFN_RESIDENT], refs[N_FFN_RESIDENT:]
    stage_res, refs = refs[:n_stage_resident], refs[n_stage_resident:]
    o_ref, acc_buf, h_scr, scratch = refs[0], refs[1], refs[2], refs[3:]
    s = pl.program_id(0)
    lag = Lag(tile_in_seq=(s + TILES_PER_SEQ - 1) % TILES_PER_SEQ,
              next_is_seq_start=s % TILES_PER_SEQ == 0)

    def ffn(ffn_pace, stage_pace, out):
        return _ffn_steps(x_ref, pre_g, wg, wu, wd, ffn_pace, stage_pace, out)

    def stage(ffn_pace, stage_pace, out):
        return stage_steps(h_scr, lag, *stage_rows, *stage_res, *scratch, ffn_pace, stage_pace,
                           out)

    @pl.when(s == 0)
    def _():
        for scr in scratch:
            scr[...] = jnp.zeros(scr.shape, scr.dtype)
        ffn_out = []
        _interleave(ffn([], [], ffn_out))
        acc_buf[...] = ffn_out[0]

    @pl.when(jnp.logical_and(s > 0, s < N_TILES))
    def _():
        h_scr[...] = _ffn_finish(x_prev_ref, acc_buf, post_g)
        ffn_pace, stage_pace, ffn_out, stage_out = [], [], [], []
        _interleave(ffn(ffn_pace, stage_pace, ffn_out), stage(ffn_pace, stage_pace, stage_out))
        o_ref[...] = stage_out[0]
        acc_buf[...] = ffn_out[0]

    @pl.when(s == N_TILES)
    def _():
        h_scr[...] = _ffn_finish(x_prev_ref, acc_buf, post_g)
        stage_out = []
        _interleave(stage([], [], stage_out))
        o_ref[...] = stage_out[0]


def _row_spec(width):
    return pl.BlockSpec((ROW_TILE, width), lambda s: (s, 0))


def _resident_spec(stacked, layer):
    zeros = (0,) * (stacked.ndim - 1)
    return pl.BlockSpec((None,) + stacked.shape[1:], lambda s: (layer,) + zeros,
                        pipeline_mode=pl.Buffered(1))


def _compiler_params():
    return pltpu.CompilerParams(dimension_semantics=("arbitrary",),
                                vmem_limit_bytes=VMEM_LIMIT_BYTES)


def _call(body, name, x, resident_inputs, scratch_shapes=()):
    return pl.pallas_call(
        body,
        name=name,
        grid=(N_TILES,),
        in_specs=[_row_spec(D_MODEL)] + [_resident_spec(a, l) for a, l in resident_inputs],
        out_specs=_row_spec(D_MODEL),
        out_shape=jax.ShapeDtypeStruct((TOKENS, D_MODEL), F32),
        scratch_shapes=list(scratch_shapes),
        compiler_params=_compiler_params(),
    )(x, *[a for a, _ in resident_inputs])


def _fused_call(stage_steps, name, x, ffn_resident, stage_resident, stage_rows=(),
                scratch_shapes=()):
    resident = list(ffn_resident) + list(stage_resident)
    in_specs = [pl.BlockSpec((ROW_TILE, D_MODEL), lambda s: (jnp.minimum(s, N_TILES - 1), 0)),
                pl.BlockSpec((ROW_TILE, D_MODEL), lambda s: (jnp.maximum(s - 1, 0), 0))]
    for a, layer in stage_rows:
        in_specs.append(pl.BlockSpec((None, ROW_TILE, a.shape[-1]),
                                     lambda s, layer=layer: (layer, jnp.maximum(s - 1, 0), 0)))
    in_specs += [_resident_spec(a, l) for a, l in resident]
    return pl.pallas_call(
        functools.partial(_ffn_then_stage_kernel, stage_steps, len(stage_rows),
                          len(stage_resident)),
        name=name,
        grid=(N_TILES + 1,),
        in_specs=in_specs,
        out_specs=pl.BlockSpec((ROW_TILE, D_MODEL), lambda s: (jnp.maximum(s - 1, 0), 0)),
        out_shape=jax.ShapeDtypeStruct((TOKENS, D_MODEL), F32),
        scratch_shapes=[pltpu.VMEM((ROW_TILE, D_MODEL), F32),
                        pltpu.VMEM((ROW_TILE, D_MODEL), F32)]
        + list(scratch_shapes),
        compiler_params=_compiler_params(),
    )(x, x, *[a for a, _ in stage_rows], *[a for a, _ in resident])


def _vecs(v):
    return v.reshape(v.shape[0], 1, v.shape[1])


def kernel(x, p, ff1_pre_g, ff1_w_gate, ff1_w_up, ff1_w_down, ff1_post_g, mix_pre_g, mix_post_g, ff2_pre_g, ff2_w_gate, ff2_w_up, ff2_w_down, ff2_post_g, ple_gate_norm_g, ple_w_gate, ple_w_proj, ple_post_g, a_w_in, a_v_norm_g, a_v_norm_b, a_w_s, a_b_s, a_w_out, b_w_grp, b_scale, c_w_pw1, c_w_dw, c_b_dw, c_norm_g, c_norm_b, c_w_pw2, d_w_in, d_w_conv, d_w_out):
    bf = lambda w: w.astype(BF16)
    ff1_w_gate, ff1_w_up, ff1_w_down = bf(ff1_w_gate), bf(ff1_w_up), bf(ff1_w_down)
    ff2_w_gate, ff2_w_up, ff2_w_down = bf(ff2_w_gate), bf(ff2_w_up), bf(ff2_w_down)
    ple_w_gate, ple_w_proj = bf(ple_w_gate), bf(ple_w_proj)
    a_w_in, a_w_out, b_w_grp = bf(a_w_in), bf(a_w_out), bf(b_w_grp)
    c_w_pw1, c_w_pw2, d_w_in, d_w_out = bf(c_w_pw1), bf(c_w_pw2), bf(d_w_in), bf(d_w_out)

    ff1_pre_g, ff1_post_g, ff2_pre_g, ff2_post_g = map(
        _vecs, (ff1_pre_g, ff1_post_g, ff2_pre_g, ff2_post_g))
    mix_pre_g, mix_post_g = _vecs(mix_pre_g), _vecs(mix_post_g)
    ple_gate_norm_g, ple_post_g = _vecs(ple_gate_norm_g), _vecs(ple_post_g)
    a_v_norm_g, a_v_norm_b, b_scale = _vecs(a_v_norm_g), _vecs(a_v_norm_b), _vecs(b_scale)
    c_b_dw, c_norm_g, c_norm_b = _vecs(c_b_dw), _vecs(c_norm_g), _vecs(c_norm_b)
    a_b_s = a_b_s.reshape(a_b_s.shape[0], A_GROUPS, CHUNK, 1)

    h = x.reshape(TOKENS, D_MODEL)
    p = p.reshape(DEPTH, TOKENS, PLE_DIM)
    for i in range(DEPTH):
        ffn1 = [(ff1_pre_g, i), (ff1_w_gate, i), (ff1_w_up, i), (ff1_w_down, i), (ff1_post_g, i)]
        m, j = i % N_MIXERS, i // N_MIXERS
        pre, post = (mix_pre_g, i), (mix_post_g, i)
        if m == 0:
            h = _call(_ffn_kernel, f"ffn1_l{i}", h, ffn1)
            h = _call(_gmlp_kernel, f"gmlp_l{i}", h,
                      [pre, (a_w_in, j), (a_v_norm_g, j), (a_v_norm_b, j), (a_w_s, j),
                       (a_b_s, j), (a_w_out, j), post],
                      [pltpu.VMEM((ROW_TILE, 2 * A_HALF), F32)])
        elif m == 1:
            h = _fused_call(_pool_steps, f"ffn1_pool_l{i}", h, ffn1,
                            [pre, (b_w_grp, j), (b_scale, j), post],
                            scratch_shapes=[pltpu.VMEM((POOL_HALO + ROW_TILE, D_MODEL), F32)])
        elif m == 2:
            h = _fused_call(_conformer_steps, f"ffn1_conformer_l{i}", h, ffn1,
                            [pre, (c_w_pw1, j), (c_w_dw, j), (c_b_dw, j), (c_norm_g, j),
                             (c_norm_b, j), (c_w_pw2, j), post],
                            scratch_shapes=[pltpu.VMEM((CONV_HALO + ROW_TILE, D_MODEL), F32)])
        else:
            h = _fused_call(_short_conv_steps, f"ffn1_shortconv_l{i}", h, ffn1,
                            [pre, (d_w_in, j), (d_w_conv, j), (d_w_out, j), post],
                            scratch_shapes=[pltpu.VMEM((SHORT_HALO + ROW_TILE, D_MODEL), F32)])
        h = _fused_call(_ple_steps, f"ffn2_ple_l{i}", h,
                        [(ff2_pre_g, i), (ff2_w_gate, i), (ff2_w_up, i), (ff2_w_down, i),
                         (ff2_post_g, i)],
                        [(ple_gate_norm_g, i), (ple_w_gate, i), (ple_w_proj, i), (ple_post_g, i)],
                        stage_rows=[(p, i)])
    return h.reshape(BATCH, SEQ, D_MODEL)
```

```python
import collections
import functools

import jax
import jax.numpy as jnp
from jax import lax
from jax.experimental import pallas as pl
from jax.experimental.pallas import tpu as pltpu

D_MODEL = 1024
BATCH = 8
SEQ = 4096
DEPTH = 4
N_MIXERS = 4
D_FF = 2816
PLE_DIM = 256
CHUNK = 128
A_HALF = 3 * D_MODEL
A_GROUPS = A_HALF // 256
A_GROUP_WIDTH = A_HALF // A_GROUPS
POOL_WINDOWS = (2, 4, 8, 16)
POOL_GROUP = D_MODEL // len(POOL_WINDOWS)
CONV_WIDTH = 31
SHORT_CONV_WIDTH = 3
EPS = 1e-6

TOKENS = BATCH * SEQ
ROW_TILE = 512
N_TILES = TOKENS // ROW_TILE
TILES_PER_SEQ = SEQ // ROW_TILE
FF_CHUNK = 256
SUBLANES = 8
LANES = 128
POOL_HALO = 16
CONV_HALO = 32
SHORT_HALO = 8
VMEM_LIMIT_BYTES = 56 * 1024 * 1024

BF16 = jnp.bfloat16
F32 = jnp.float32


def _dot(a, b):
    return jnp.dot(a, b, preferred_element_type=F32)


def _rms(x, g):
    return x * lax.rsqrt(jnp.mean(x * x, axis=-1, keepdims=True) + EPS) * g


def _layer_norm(x, g, b):
    mu = jnp.mean(x, axis=-1, keepdims=True)
    xc = x - mu
    return xc * lax.rsqrt(jnp.mean(xc * xc, axis=-1, keepdims=True) + EPS) * g + b


def _swiglu_tile(xn, wg_ref, wu_ref, wd_ref):
    acc = jnp.zeros((xn.shape[0], D_MODEL), F32)
    for c in range(D_FF // FF_CHUNK):
        cols = slice(c * FF_CHUNK, (c + 1) * FF_CHUNK)
        gate = _dot(xn, wg_ref[:, cols])
        up = _dot(xn, wu_ref[:, cols])
        hid = (gate * jax.nn.sigmoid(gate) * up).astype(BF16)
        acc = acc + _dot(hid, wd_ref[cols, :])
    return acc


def _gmlp_kernel(x_ref, pre_g, w_in, v_g, v_b, w_s, b_s, w_out, post_g, o_ref, z_scr):
    hn = _rms(x_ref[...], pre_g[...]).astype(BF16)
    for g in range(2 * A_GROUPS):
        cols = slice(g * A_GROUP_WIDTH, (g + 1) * A_GROUP_WIDTH)
        z_scr[:, cols] = jax.nn.gelu(_dot(hn, w_in[:, cols]))
    vn = _layer_norm(z_scr[:, A_HALF:2 * A_HALF], v_g[...], v_b[...]).astype(BF16)
    row = lax.broadcasted_iota(jnp.int32, (CHUNK, CHUNK), 0)
    col = lax.broadcasted_iota(jnp.int32, (CHUNK, CHUNK), 1)
    causal = col <= row
    acc = jnp.zeros((ROW_TILE, D_MODEL), F32)
    for g in range(A_GROUPS):
        cols = slice(g * A_GROUP_WIDTH, (g + 1) * A_GROUP_WIDTH)
        ws = jnp.where(causal, w_s[g], 0.0).astype(BF16)
        bias = b_s[g]
        sv = jnp.concatenate(
            [_dot(ws, vn[n * CHUNK:(n + 1) * CHUNK, cols]) + bias
             for n in range(ROW_TILE // CHUNK)], axis=0)
        acc = acc + _dot((z_scr[:, cols] * sv).astype(BF16), w_out[cols, :])
    o_ref[...] = x_ref[...] + _rms(acc, post_g[...])


Lag = collections.namedtuple("Lag", ["tile_in_seq", "next_is_seq_start"])


def _zero_after(value):
    bits = pltpu.bitcast(value[0:SUBLANES, 0:LANES], jnp.uint32)
    bits = lax.shift_right_logical(lax.shift_right_logical(bits, jnp.uint32(16)), jnp.uint32(16))
    return pltpu.bitcast(bits, F32)[0:1, 0:1]


def _after(value, pace):
    if not pace:
        return value
    return value + _zero_after(pace[-1]).astype(value.dtype)


def _interleave(*gens):
    live = list(gens)
    while live:
        for g in list(live):
            try:
                next(g)
            except StopIteration:
                live.remove(g)


def _ffn_steps(x_ref, pre_g, wg, wu, wd, post_g, ffn_pace, other_pace, result):
    xn = _rms(x_ref[...], pre_g[...]).astype(BF16)
    acc = jnp.zeros((ROW_TILE, D_MODEL), F32)
    yield
    for c in range(D_FF // FF_CHUNK):
        cols = slice(c * FF_CHUNK, (c + 1) * FF_CHUNK)
        xc = _after(xn, other_pace)
        gate = _dot(xc, wg[:, cols])
        up = _dot(xc, wu[:, cols])
        hid = (gate * jax.nn.sigmoid(gate) * up).astype(BF16)
        acc = acc + _dot(hid, wd[cols, :])
        ffn_pace.append(acc)
        yield
    result.append(x_ref[...] + 0.5 * _rms(acc, post_g[...]))


def _causal_taps(ext_val, halo, w_ref, cols, init):
    n_taps = w_ref.shape[0]
    first = halo - (n_taps - 1)
    ext_rows = halo + ROW_TILE
    out = init
    for r in range(SUBLANES):
        taps = [k for k in range(n_taps) if (first + k) % SUBLANES == r]
        if not taps:
            continue
        shifted = ext_val if r == 0 else pltpu.roll(ext_val, ext_rows - r, 0)
        for k in taps:
            base = first + k - r
            out = out + shifted[base:base + ROW_TILE, :] * w_ref[k:k + 1, cols]
    return out


def _pool_steps(h_ref, lag, pre_g, w_grp, scale, post_g, ext, ffn_pace, own_pace, result):
    ext[POOL_HALO:POOL_HALO + ROW_TILE, :] = _rms(h_ref[...], pre_g[...])
    yield
    pos = lag.tile_in_seq * ROW_TILE + lax.broadcasted_iota(jnp.int32, (ROW_TILE, 1), 0)
    outs = []
    for g, w in enumerate(POOL_WINDOWS):
        cols = slice(g * POOL_GROUP, (g + 1) * POOL_GROUP)
        total = _after(ext[:, cols], ffn_pace)
        span = 1
        while span < w:
            total = total + pltpu.roll(total, span, 0)
            span *= 2
        total = total[POOL_HALO:POOL_HALO + ROW_TILE, :]
        count = jnp.minimum(pos + 1, w).astype(F32)
        pooled = (total / count - ext[POOL_HALO:POOL_HALO + ROW_TILE, cols]).astype(BF16)
        outs.append(_dot(pooled, w_grp[g]))
        own_pace.append(outs[-1])
        yield
    y = jnp.concatenate(outs, axis=-1) * scale[...]
    tail = ext[ROW_TILE:ROW_TILE + POOL_HALO, :]
    ext[0:POOL_HALO, :] = jnp.where(lag.next_is_seq_start, 0.0, tail)
    result.append(h_ref[...] + _rms(y, post_g[...]))


def _conformer_steps(h_ref, lag, pre_g, w_pw1, w_dw, b_dw, n_g, n_b, w_pw2, post_g, ext,
                     ffn_pace, own_pace, result):
    hn = _rms(h_ref[...], pre_g[...]).astype(BF16)
    a = _dot(hn, w_pw1[:, 0:D_MODEL])
    gate = _dot(hn, w_pw1[:, D_MODEL:2 * D_MODEL])
    ext[CONV_HALO:CONV_HALO + ROW_TILE, :] = a * jax.nn.sigmoid(gate)
    yield
    conv_cols = []
    for cb in range(D_MODEL // LANES):
        cols = slice(cb * LANES, (cb + 1) * LANES)
        bias = jnp.zeros((ROW_TILE, LANES), F32) + b_dw[:, cols]
        conv = _causal_taps(_after(ext[:, cols], ffn_pace), CONV_HALO, w_dw, cols, bias)
        conv_cols.append(conv)
        own_pace.append(conv)
        yield
    z = _layer_norm(_after(jnp.concatenate(conv_cols, axis=-1), ffn_pace), n_g[...], n_b[...])
    z = (z * jax.nn.sigmoid(z)).astype(BF16)
    yield
    y = _dot(z, w_pw2[...])
    yield
    tail = ext[ROW_TILE:ROW_TILE + CONV_HALO, :]
    ext[0:CONV_HALO, :] = jnp.where(lag.next_is_seq_start, 0.0, tail)
    result.append(h_ref[...] + _rms(y, post_g[...]))


def _short_conv_steps(h_ref, lag, pre_g, w_in, w_conv, w_out, post_g, ext,
                      ffn_pace, own_pace, result):
    hn = _rms(h_ref[...], pre_g[...]).astype(BF16)
    cg = _dot(hn, w_in[:, D_MODEL:2 * D_MODEL])
    xv = _dot(hn, w_in[:, 2 * D_MODEL:3 * D_MODEL])
    ext[SHORT_HALO:SHORT_HALO + ROW_TILE, :] = cg * xv
    yield
    bg = _dot(_after(hn, ffn_pace), w_in[:, 0:D_MODEL])
    own_pace.append(bg)
    yield
    gated_cols = []
    for cb in range(D_MODEL // LANES):
        cols = slice(cb * LANES, (cb + 1) * LANES)
        conv = _causal_taps(_after(ext[:, cols], ffn_pace), SHORT_HALO, w_conv, cols,
                            jnp.zeros((ROW_TILE, LANES), F32))
        gated_cols.append((bg[:, cols] * conv).astype(BF16))
        own_pace.append(conv)
        yield
    y = _dot(jnp.concatenate(gated_cols, axis=-1), w_out[...])
    yield
    tail = ext[ROW_TILE:ROW_TILE + SHORT_HALO, :]
    ext[0:SHORT_HALO, :] = jnp.where(lag.next_is_seq_start, 0.0, tail)
    result.append(h_ref[...] + _rms(y, post_g[...]))


def _ple_steps(h_ref, lag, p_ref, gn_g, w_gate, w_proj, post_g, ffn_pace, own_pace, result):
    gate_in = _rms(h_ref[...], gn_g[...]).astype(BF16)
    gate_lin = _dot(gate_in, w_gate[...])
    yield
    proj = _dot(p_ref[...].astype(BF16), w_proj[...])
    e = proj * jax.nn.sigmoid(_after(gate_lin, ffn_pace))
    own_pace.append(e)
    yield
    result.append(h_ref[...] + _rms(_after(e, ffn_pace), post_g[...]))


FFN_WIDE_ROWS = 128
FFN_NARROW_ROWS = 256
N_FFN_SCRATCH = 7


def _ffn_weight_scratch():
    return [pltpu.VMEM((D_MODEL, D_FF), BF16),
            pltpu.VMEM((D_MODEL, D_FF), BF16),
            pltpu.VMEM((D_FF, D_MODEL), BF16),
            pltpu.VMEM((2, FFN_WIDE_ROWS, D_FF), F32),
            pltpu.VMEM((2, FFN_NARROW_ROWS, D_MODEL), F32),
            pltpu.SemaphoreType.DMA((2,)),
            pltpu.SemaphoreType.DMA((2,))]


def _fetch_as_bf16(src, layer, dst, staging, sem):
    rows = staging.shape[1]
    assert dst.shape[0] % rows == 0
    n_chunks = dst.shape[0] // rows

    def copy(i):
        return pltpu.make_async_copy(src.at[layer, pl.ds(i * rows, rows), :],
                                     staging.at[i % 2], sem.at[i % 2])

    copy(0).start()
    for i in range(n_chunks):
        if i + 1 < n_chunks:
            copy(i + 1).start()
        copy(i).wait()
        dst[pl.ds(i * rows, rows), :] = staging[i % 2].astype(BF16)


def _fetch_ffn_weights(layer, wg_hbm, wu_hbm, wd_hbm, ffn_scratch):
    wg, wu, wd, wide, narrow, sem_wide, sem_narrow = ffn_scratch
    _fetch_as_bf16(wg_hbm, layer, wg, wide, sem_wide)
    _fetch_as_bf16(wu_hbm, layer, wu, wide, sem_wide)
    _fetch_as_bf16(wd_hbm, layer, wd, narrow, sem_narrow)


def _ffn_kernel(layer, x_ref, pre_g, wg_hbm, wu_hbm, wd_hbm, post_g, o_ref, *ffn_scratch):
    @pl.when(pl.program_id(0) == 0)
    def _():
        _fetch_ffn_weights(layer, wg_hbm, wu_hbm, wd_hbm, ffn_scratch)

    wg, wu, wd = ffn_scratch[:3]
    x = x_ref[...]
    xn = _rms(x, pre_g[...]).astype(BF16)
    f = _swiglu_tile(xn, wg, wu, wd)
    o_ref[...] = x + 0.5 * _rms(f, post_g[...])


N_FFN_INPUTS = 5


def _ffn_then_stage_kernel(stage_steps, layer, n_stage_rows, n_stage_resident, x_ref, *refs):
    stage_rows, refs = refs[:n_stage_rows], refs[n_stage_rows:]
    (pre_g, wg_hbm, wu_hbm, wd_hbm, post_g), refs = refs[:N_FFN_INPUTS], refs[N_FFN_INPUTS:]
    stage_res, refs = refs[:n_stage_resident], refs[n_stage_resident:]
    o_ref, hbuf, refs = refs[0], refs[1], refs[2:]
    ffn_scratch, scratch = refs[:N_FFN_SCRATCH], refs[N_FFN_SCRATCH:]
    wg, wu, wd = ffn_scratch[:3]
    s = pl.program_id(0)

    @pl.when(s == 0)
    def _():
        _fetch_ffn_weights(layer, wg_hbm, wu_hbm, wd_hbm, ffn_scratch)
        hbuf[...] = jnp.zeros(hbuf.shape, F32)
        for scr in scratch:
            scr[...] = jnp.zeros(scr.shape, scr.dtype)

    lag = Lag(tile_in_seq=(s + TILES_PER_SEQ - 1) % TILES_PER_SEQ,
              next_is_seq_start=s % TILES_PER_SEQ == 0)
    ffn_pace, stage_pace, ffn_out, stage_out = [], [], [], []
    _interleave(
        _ffn_steps(x_ref, pre_g, wg, wu, wd, post_g, ffn_pace, stage_pace, ffn_out),
        stage_steps(hbuf, lag, *stage_rows, *stage_res, *scratch, ffn_pace, stage_pace,
                    stage_out))
    o_ref[...] = stage_out[0]
    hbuf[...] = ffn_out[0]


def _row_spec(width):
    return pl.BlockSpec((ROW_TILE, width), lambda s: (s, 0))


def _resident_spec(stacked, layer):
    zeros = (0,) * (stacked.ndim - 1)
    return pl.BlockSpec((None,) + stacked.shape[1:], lambda s: (layer,) + zeros,
                        pipeline_mode=pl.Buffered(1))


def _compiler_params():
    return pltpu.CompilerParams(dimension_semantics=("arbitrary",),
                                vmem_limit_bytes=VMEM_LIMIT_BYTES)


def _call(body, name, x, resident_inputs, scratch_shapes=()):
    return pl.pallas_call(
        body,
        name=name,
        grid=(N_TILES,),
        in_specs=[_row_spec(D_MODEL)] + [_resident_spec(a, l) for a, l in resident_inputs],
        out_specs=_row_spec(D_MODEL),
        out_shape=jax.ShapeDtypeStruct((TOKENS, D_MODEL), F32),
        scratch_shapes=list(scratch_shapes),
        compiler_params=_compiler_params(),
    )(x, *[a for a, _ in resident_inputs])


def _ffn_specs(ffn, layer):
    pre_g, _, _, _, post_g = ffn
    hbm = pl.BlockSpec(memory_space=pl.ANY)
    return [_resident_spec(pre_g, layer), hbm, hbm, hbm, _resident_spec(post_g, layer)]


def _ffn_call(name, x, ffn, layer):
    return pl.pallas_call(
        functools.partial(_ffn_kernel, layer),
        name=name,
        grid=(N_TILES,),
        in_specs=[_row_spec(D_MODEL)] + _ffn_specs(ffn, layer),
        out_specs=_row_spec(D_MODEL),
        out_shape=jax.ShapeDtypeStruct((TOKENS, D_MODEL), F32),
        scratch_shapes=_ffn_weight_scratch(),
        compiler_params=_compiler_params(),
    )(x, *ffn)


def _fused_call(stage_steps, name, x, ffn, layer, stage_resident, stage_rows=(),
                scratch_shapes=()):
    in_specs = [pl.BlockSpec((ROW_TILE, D_MODEL), lambda s: (jnp.minimum(s, N_TILES - 1), 0))]
    for a, row_layer in stage_rows:
        in_specs.append(pl.BlockSpec(
            (None, ROW_TILE, a.shape[-1]),
            lambda s, row_layer=row_layer: (row_layer, jnp.maximum(s - 1, 0), 0)))
    in_specs += _ffn_specs(ffn, layer)
    in_specs += [_resident_spec(a, l) for a, l in stage_resident]
    return pl.pallas_call(
        functools.partial(_ffn_then_stage_kernel, stage_steps, layer, len(stage_rows),
                          len(stage_resident)),
        name=name,
        grid=(N_TILES + 1,),
        in_specs=in_specs,
        out_specs=pl.BlockSpec((ROW_TILE, D_MODEL), lambda s: (jnp.maximum(s - 1, 0), 0)),
        out_shape=jax.ShapeDtypeStruct((TOKENS, D_MODEL), F32),
        scratch_shapes=[pltpu.VMEM((ROW_TILE, D_MODEL), F32)] + _ffn_weight_scratch()
        + list(scratch_shapes),
        compiler_params=_compiler_params(),
    )(x, *[a for a, _ in stage_rows], *ffn, *[a for a, _ in stage_resident])


def _vecs(v):
    return v.reshape(v.shape[0], 1, v.shape[1])


def kernel(x, p, ff1_pre_g, ff1_w_gate, ff1_w_up, ff1_w_down, ff1_post_g, mix_pre_g, mix_post_g, ff2_pre_g, ff2_w_gate, ff2_w_up, ff2_w_down, ff2_post_g, ple_gate_norm_g, ple_w_gate, ple_w_proj, ple_post_g, a_w_in, a_v_norm_g, a_v_norm_b, a_w_s, a_b_s, a_w_out, b_w_grp, b_scale, c_w_pw1, c_w_dw, c_b_dw, c_norm_g, c_norm_b, c_w_pw2, d_w_in, d_w_conv, d_w_out):
    bf = lambda w: w.astype(BF16)
    ple_w_gate, ple_w_proj = bf(ple_w_gate), bf(ple_w_proj)
    a_w_in, a_w_out, b_w_grp = bf(a_w_in), bf(a_w_out), bf(b_w_grp)
    c_w_pw1, c_w_pw2, d_w_in, d_w_out = bf(c_w_pw1), bf(c_w_pw2), bf(d_w_in), bf(d_w_out)

    ff1_pre_g, ff1_post_g, ff2_pre_g, ff2_post_g = map(
        _vecs, (ff1_pre_g, ff1_post_g, ff2_pre_g, ff2_post_g))
    mix_pre_g, mix_post_g = _vecs(mix_pre_g), _vecs(mix_post_g)
    ple_gate_norm_g, ple_post_g = _vecs(ple_gate_norm_g), _vecs(ple_post_g)
    a_v_norm_g, a_v_norm_b, b_scale = _vecs(a_v_norm_g), _vecs(a_v_norm_b), _vecs(b_scale)
    c_b_dw, c_norm_g, c_norm_b = _vecs(c_b_dw), _vecs(c_norm_g), _vecs(c_norm_b)
    a_b_s = a_b_s.reshape(a_b_s.shape[0], A_GROUPS, CHUNK, 1)

    h = x.reshape(TOKENS, D_MODEL)
    p = p.reshape(DEPTH, TOKENS, PLE_DIM)
    ffn1 = (ff1_pre_g, ff1_w_gate, ff1_w_up, ff1_w_down, ff1_post_g)
    ffn2 = (ff2_pre_g, ff2_w_gate, ff2_w_up, ff2_w_down, ff2_post_g)
    for i in range(DEPTH):
        m, j = i % N_MIXERS, i // N_MIXERS
        pre, post = (mix_pre_g, i), (mix_post_g, i)
        if m == 0:
            h = _ffn_call(f"ffn1_l{i}", h, ffn1, i)
            h = _call(_gmlp_kernel, f"gmlp_l{i}", h,
                      [pre, (a_w_in, j), (a_v_norm_g, j), (a_v_norm_b, j), (a_w_s, j),
                       (a_b_s, j), (a_w_out, j), post],
                      [pltpu.VMEM((ROW_TILE, 2 * A_HALF), F32)])
        elif m == 1:
            h = _fused_call(_pool_steps, f"ffn1_pool_l{i}", h, ffn1, i,
                            [pre, (b_w_grp, j), (b_scale, j), post],
                            scratch_shapes=[pltpu.VMEM((POOL_HALO + ROW_TILE, D_MODEL), F32)])
        elif m == 2:
            h = _fused_call(_conformer_steps, f"ffn1_conformer_l{i}", h, ffn1, i,
                            [pre, (c_w_pw1, j), (c_w_dw, j), (c_b_dw, j), (c_norm_g, j),
                             (c_norm_b, j), (c_w_pw2, j), post],
                            scratch_shapes=[pltpu.VMEM((CONV_HALO + ROW_TILE, D_MODEL), F32)])
        else:
            h = _fused_call(_short_conv_steps, f"ffn1_shortconv_l{i}", h, ffn1, i,
                            [pre, (d_w_in, j), (d_w_conv, j), (d_w_out, j), post],
                            scratch_shapes=[pltpu.VMEM((SHORT_HALO + ROW_TILE, D_MODEL), F32)])
        h = _fused_call(_ple_steps, f"ffn2_ple_l{i}", h, ffn2, i,
                        [(ple_gate_norm_g, i), (ple_w_gate, i), (ple_w_proj, i), (ple_post_g, i)],
                        stage_rows=[(p, i)])
    return h.reshape(BATCH, SEQ, D_MODEL)
```

```python
import collections
import functools

import jax
import jax.numpy as jnp
from jax import lax
from jax.experimental import pallas as pl
from jax.experimental.pallas import tpu as pltpu

D_MODEL = 1024
BATCH = 8
SEQ = 4096
DEPTH = 4
N_MIXERS = 4
D_FF = 2816
PLE_DIM = 256
CHUNK = 128
A_HALF = 3 * D_MODEL
A_GROUPS = A_HALF // 256
A_GROUP_WIDTH = A_HALF // A_GROUPS
POOL_WINDOWS = (2, 4, 8, 16)
POOL_GROUP = D_MODEL // len(POOL_WINDOWS)
CONV_WIDTH = 31
SHORT_CONV_WIDTH = 3
EPS = 1e-6

TOKENS = BATCH * SEQ
ROW_TILE = 512
N_TILES = TOKENS // ROW_TILE
TILES_PER_SEQ = SEQ // ROW_TILE
FF_CHUNK = 256
SUBLANES = 8
LANES = 128
POOL_HALO = 16
CONV_HALO = 32
SHORT_HALO = 8
VMEM_LIMIT_BYTES = 56 * 1024 * 1024

BF16 = jnp.bfloat16
F32 = jnp.float32


def _dot(a, b):
    return jnp.dot(a, b, preferred_element_type=F32)


def _rms(x, g):
    return x * lax.rsqrt(jnp.mean(x * x, axis=-1, keepdims=True) + EPS) * g


def _layer_norm(x, g, b):
    mu = jnp.mean(x, axis=-1, keepdims=True)
    xc = x - mu
    return xc * lax.rsqrt(jnp.mean(xc * xc, axis=-1, keepdims=True) + EPS) * g + b


def _swiglu_tile(xn, wg_ref, wu_ref, wd_ref):
    acc = jnp.zeros((xn.shape[0], D_MODEL), F32)
    for c in range(D_FF // FF_CHUNK):
        cols = slice(c * FF_CHUNK, (c + 1) * FF_CHUNK)
        gate = _dot(xn, wg_ref[:, cols])
        up = _dot(xn, wu_ref[:, cols])
        hid = (gate * jax.nn.sigmoid(gate) * up).astype(BF16)
        acc = acc + _dot(hid, wd_ref[cols, :])
    return acc


def _gmlp_kernel(x_ref, pre_g, w_in, v_g, v_b, w_s, b_s, w_out, post_g, o_ref, z_scr):
    hn = _rms(x_ref[...], pre_g[...]).astype(BF16)
    for g in range(2 * A_GROUPS):
        cols = slice(g * A_GROUP_WIDTH, (g + 1) * A_GROUP_WIDTH)
        z_scr[:, cols] = jax.nn.gelu(_dot(hn, w_in[:, cols]))
    vn = _layer_norm(z_scr[:, A_HALF:2 * A_HALF], v_g[...], v_b[...]).astype(BF16)
    row = lax.broadcasted_iota(jnp.int32, (CHUNK, CHUNK), 0)
    col = lax.broadcasted_iota(jnp.int32, (CHUNK, CHUNK), 1)
    causal = col <= row
    acc = jnp.zeros((ROW_TILE, D_MODEL), F32)
    for g in range(A_GROUPS):
        cols = slice(g * A_GROUP_WIDTH, (g + 1) * A_GROUP_WIDTH)
        ws = jnp.where(causal, w_s[g], 0.0).astype(BF16)
        bias = b_s[g]
        sv = jnp.concatenate(
            [_dot(ws, vn[n * CHUNK:(n + 1) * CHUNK, cols]) + bias
             for n in range(ROW_TILE // CHUNK)], axis=0)
        acc = acc + _dot((z_scr[:, cols] * sv).astype(BF16), w_out[cols, :])
    o_ref[...] = x_ref[...] + _rms(acc, post_g[...])


Lag = collections.namedtuple("Lag", ["tile_in_seq", "next_is_seq_start"])


def _zero_after(value):
    bits = pltpu.bitcast(value[0:SUBLANES, 0:LANES], jnp.uint32)
    bits = lax.shift_right_logical(lax.shift_right_logical(bits, jnp.uint32(16)), jnp.uint32(16))
    return pltpu.bitcast(bits, F32)[0:1, 0:1]


def _after(value, pace):
    if not pace:
        return value
    return value + _zero_after(pace[-1]).astype(value.dtype)


def _interleave(*gens):
    live = list(gens)
    while live:
        for g in list(live):
            try:
                next(g)
            except StopIteration:
                live.remove(g)


def _ffn_steps(x_ref, pre_g, wg, wu, wd, post_g, ffn_pace, other_pace, result):
    xn = _rms(x_ref[...], pre_g[...]).astype(BF16)
    acc = jnp.zeros((ROW_TILE, D_MODEL), F32)
    yield
    for c in range(D_FF // FF_CHUNK):
        cols = slice(c * FF_CHUNK, (c + 1) * FF_CHUNK)
        xc = _after(xn, other_pace)
        gate = _dot(xc, wg[:, cols])
        up = _dot(xc, wu[:, cols])
        hid = (gate * jax.nn.sigmoid(gate) * up).astype(BF16)
        acc = acc + _dot(hid, wd[cols, :])
        ffn_pace.append(acc)
        yield
    result.append(x_ref[...] + 0.5 * _rms(acc, post_g[...]))


def _causal_taps(ext_val, halo, w_ref, cols, init):
    n_taps = w_ref.shape[0]
    first = halo - (n_taps - 1)
    ext_rows = halo + ROW_TILE
    out = init
    for r in range(SUBLANES):
        taps = [k for k in range(n_taps) if (first + k) % SUBLANES == r]
        if not taps:
            continue
        shifted = ext_val if r == 0 else pltpu.roll(ext_val, ext_rows - r, 0)
        for k in taps:
            base = first + k - r
            out = out + shifted[base:base + ROW_TILE, :] * w_ref[k:k + 1, cols]
    return out


def _pool_steps(h_ref, lag, pre_g, w_grp, scale, post_g, ext, ffn_pace, own_pace, result):
    ext[POOL_HALO:POOL_HALO + ROW_TILE, :] = _rms(h_ref[...], pre_g[...])
    yield
    pos = lag.tile_in_seq * ROW_TILE + lax.broadcasted_iota(jnp.int32, (ROW_TILE, 1), 0)
    outs = []
    for g, w in enumerate(POOL_WINDOWS):
        cols = slice(g * POOL_GROUP, (g + 1) * POOL_GROUP)
        total = _after(ext[:, cols], ffn_pace)
        span = 1
        while span < w:
            total = total + pltpu.roll(total, span, 0)
            span *= 2
        total = total[POOL_HALO:POOL_HALO + ROW_TILE, :]
        count = jnp.minimum(pos + 1, w).astype(F32)
        pooled = (total / count - ext[POOL_HALO:POOL_HALO + ROW_TILE, cols]).astype(BF16)
        outs.append(_dot(pooled, w_grp[g]))
        own_pace.append(outs[-1])
        yield
    y = jnp.concatenate(outs, axis=-1) * scale[...]
    tail = ext[ROW_TILE:ROW_TILE + POOL_HALO, :]
    ext[0:POOL_HALO, :] = jnp.where(lag.next_is_seq_start, 0.0, tail)
    result.append(h_ref[...] + _rms(y, post_g[...]))


def _conformer_steps(h_ref, lag, pre_g, w_pw1, w_dw, b_dw, n_g, n_b, w_pw2, post_g, ext,
                     ffn_pace, own_pace, result):
    hn = _rms(h_ref[...], pre_g[...]).astype(BF16)
    a = _dot(hn, w_pw1[:, 0:D_MODEL])
    gate = _dot(hn, w_pw1[:, D_MODEL:2 * D_MODEL])
    ext[CONV_HALO:CONV_HALO + ROW_TILE, :] = a * jax.nn.sigmoid(gate)
    yield
    conv_cols = []
    for cb in range(D_MODEL // LANES):
        cols = slice(cb * LANES, (cb + 1) * LANES)
        bias = jnp.zeros((ROW_TILE, LANES), F32) + b_dw[:, cols]
        conv = _causal_taps(_after(ext[:, cols], ffn_pace), CONV_HALO, w_dw, cols, bias)
        conv_cols.append(conv)
        own_pace.append(conv)
        yield
    z = _layer_norm(_after(jnp.concatenate(conv_cols, axis=-1), ffn_pace), n_g[...], n_b[...])
    z = (z * jax.nn.sigmoid(z)).astype(BF16)
    yield
    y = _dot(z, w_pw2[...])
    yield
    tail = ext[ROW_TILE:ROW_TILE + CONV_HALO, :]
    ext[0:CONV_HALO, :] = jnp.where(lag.next_is_seq_start, 0.0, tail)
    result.append(h_ref[...] + _rms(y, post_g[...]))


def _short_conv_steps(h_ref, lag, pre_g, w_in, w_conv, w_out, post_g, ext,
                      ffn_pace, own_pace, result):
    hn = _rms(h_ref[...], pre_g[...]).astype(BF16)
    cg = _dot(hn, w_in[:, D_MODEL:2 * D_MODEL])
    xv = _dot(hn, w_in[:, 2 * D_MODEL:3 * D_MODEL])
    ext[SHORT_HALO:SHORT_HALO + ROW_TILE, :] = cg * xv
    yield
    bg = _dot(_after(hn, ffn_pace), w_in[:, 0:D_MODEL])
    own_pace.append(bg)
    yield
    gated_cols = []
    for cb in range(D_MODEL // LANES):
        cols = slice(cb * LANES, (cb + 1) * LANES)
        conv = _causal_taps(_after(ext[:, cols], ffn_pace), SHORT_HALO, w_conv, cols,
                            jnp.zeros((ROW_TILE, LANES), F32))
        gated_cols.append((bg[:, cols] * conv).astype(BF16))
        own_pace.append(conv)
        yield
    y = _dot(jnp.concatenate(gated_cols, axis=-1), w_out[...])
    yield
    tail = ext[ROW_TILE:ROW_TILE + SHORT_HALO, :]
    ext[0:SHORT_HALO, :] = jnp.where(lag.next_is_seq_start, 0.0, tail)
    result.append(h_ref[...] + _rms(y, post_g[...]))


def _ple_steps(h_ref, lag, p_ref, gn_g, w_gate, w_proj, post_g, ffn_pace, own_pace, result):
    gate_in = _rms(h_ref[...], gn_g[...]).astype(BF16)
    gate_lin = _dot(gate_in, w_gate[...])
    yield
    proj = _dot(p_ref[...].astype(BF16), w_proj[...])
    e = proj * jax.nn.sigmoid(_after(gate_lin, ffn_pace))
    own_pace.append(e)
    yield
    result.append(h_ref[...] + _rms(_after(e, ffn_pace), post_g[...]))


FFN_WIDE_ROWS = 128
FFN_NARROW_ROWS = 256
FFN_STAGING_SLOTS = 3
N_FFN_SCRATCH = 7


def _ffn_weight_scratch():
    return [pltpu.VMEM((D_MODEL, D_FF), BF16),
            pltpu.VMEM((D_MODEL, D_FF), BF16),
            pltpu.VMEM((D_FF, D_MODEL), BF16),
            pltpu.VMEM((FFN_STAGING_SLOTS, FFN_WIDE_ROWS, D_FF), F32),
            pltpu.VMEM((FFN_STAGING_SLOTS, FFN_NARROW_ROWS, D_MODEL), F32),
            pltpu.SemaphoreType.DMA((FFN_STAGING_SLOTS,)),
            pltpu.SemaphoreType.DMA((FFN_STAGING_SLOTS,))]


def _fetch_as_bf16(src, layer, dst, staging, sem):
    slots, rows = staging.shape[0], staging.shape[1]
    assert dst.shape[0] % rows == 0
    n_chunks = dst.shape[0] // rows
    ahead = slots - 1

    def copy(i):
        return pltpu.make_async_copy(src.at[layer, pl.ds(i * rows, rows), :],
                                     staging.at[i % slots], sem.at[i % slots])

    for i in range(min(ahead, n_chunks)):
        copy(i).start()
    for i in range(n_chunks):
        if i + ahead < n_chunks:
            copy(i + ahead).start()
        copy(i).wait()
        dst[pl.ds(i * rows, rows), :] = staging[i % slots].astype(BF16)


def _fetch_ffn_weights(layer, wg_hbm, wu_hbm, wd_hbm, ffn_scratch):
    wg, wu, wd, wide, narrow, sem_wide, sem_narrow = ffn_scratch
    _fetch_as_bf16(wg_hbm, layer, wg, wide, sem_wide)
    _fetch_as_bf16(wu_hbm, layer, wu, wide, sem_wide)
    _fetch_as_bf16(wd_hbm, layer, wd, narrow, sem_narrow)


def _ffn_kernel(layer, x_ref, pre_g, wg_hbm, wu_hbm, wd_hbm, post_g, o_ref, *ffn_scratch):
    @pl.when(pl.program_id(0) == 0)
    def _():
        _fetch_ffn_weights(layer, wg_hbm, wu_hbm, wd_hbm, ffn_scratch)

    wg, wu, wd = ffn_scratch[:3]
    x = x_ref[...]
    xn = _rms(x, pre_g[...]).astype(BF16)
    f = _swiglu_tile(xn, wg, wu, wd)
    o_ref[...] = x + 0.5 * _rms(f, post_g[...])


N_FFN_INPUTS = 5


def _ffn_then_stage_kernel(stage_steps, layer, n_stage_rows, n_stage_resident, x_ref, *refs):
    stage_rows, refs = refs[:n_stage_rows], refs[n_stage_rows:]
    (pre_g, wg_hbm, wu_hbm, wd_hbm, post_g), refs = refs[:N_FFN_INPUTS], refs[N_FFN_INPUTS:]
    stage_res, refs = refs[:n_stage_resident], refs[n_stage_resident:]
    o_ref, hbuf, refs = refs[0], refs[1], refs[2:]
    ffn_scratch, scratch = refs[:N_FFN_SCRATCH], refs[N_FFN_SCRATCH:]
    wg, wu, wd = ffn_scratch[:3]
    s = pl.program_id(0)

    @pl.when(s == 0)
    def _():
        _fetch_ffn_weights(layer, wg_hbm, wu_hbm, wd_hbm, ffn_scratch)
        hbuf[...] = jnp.zeros(hbuf.shape, F32)
        for scr in scratch:
            scr[...] = jnp.zeros(scr.shape, scr.dtype)

    lag = Lag(tile_in_seq=(s + TILES_PER_SEQ - 1) % TILES_PER_SEQ,
              next_is_seq_start=s % TILES_PER_SEQ == 0)
    ffn_pace, stage_pace, ffn_out, stage_out = [], [], [], []
    _interleave(
        _ffn_steps(x_ref, pre_g, wg, wu, wd, post_g, ffn_pace, stage_pace, ffn_out),
        stage_steps(hbuf, lag, *stage_rows, *stage_res, *scratch, ffn_pace, stage_pace,
                    stage_out))
    o_ref[...] = stage_out[0]
    hbuf[...] = ffn_out[0]


def _row_spec(width):
    return pl.BlockSpec((ROW_TILE, width), lambda s: (s, 0))


def _resident_spec(stacked, layer):
    zeros = (0,) * (stacked.ndim - 1)
    return pl.BlockSpec((None,) + stacked.shape[1:], lambda s: (layer,) + zeros,
                        pipeline_mode=pl.Buffered(1))


def _compiler_params():
    return pltpu.CompilerParams(dimension_semantics=("arbitrary",),
                                vmem_limit_bytes=VMEM_LIMIT_BYTES)


def _call(body, name, x, resident_inputs, scratch_shapes=()):
    return pl.pallas_call(
        body,
        name=name,
        grid=(N_TILES,),
        in_specs=[_row_spec(D_MODEL)] + [_resident_spec(a, l) for a, l in resident_inputs],
        out_specs=_row_spec(D_MODEL),
        out_shape=jax.ShapeDtypeStruct((TOKENS, D_MODEL), F32),
        scratch_shapes=list(scratch_shapes),
        compiler_params=_compiler_params(),
    )(x, *[a for a, _ in resident_inputs])


def _ffn_specs(ffn, layer):
    pre_g, _, _, _, post_g = ffn
    hbm = pl.BlockSpec(memory_space=pl.ANY)
    return [_resident_spec(pre_g, layer), hbm, hbm, hbm, _resident_spec(post_g, layer)]


def _ffn_call(name, x, ffn, layer):
    return pl.pallas_call(
        functools.partial(_ffn_kernel, layer),
        name=name,
        grid=(N_TILES,),
        in_specs=[_row_spec(D_MODEL)] + _ffn_specs(ffn, layer),
        out_specs=_row_spec(D_MODEL),
        out_shape=jax.ShapeDtypeStruct((TOKENS, D_MODEL), F32),
        scratch_shapes=_ffn_weight_scratch(),
        compiler_params=_compiler_params(),
    )(x, *ffn)


def _fused_call(stage_steps, name, x, ffn, layer, stage_resident, stage_rows=(),
                scratch_shapes=()):
    in_specs = [pl.BlockSpec((ROW_TILE, D_MODEL), lambda s: (jnp.minimum(s, N_TILES - 1), 0))]
    for a, row_layer in stage_rows:
        in_specs.append(pl.BlockSpec(
            (None, ROW_TILE, a.shape[-1]),
            lambda s, row_layer=row_layer: (row_layer, jnp.maximum(s - 1, 0), 0)))
    in_specs += _ffn_specs(ffn, layer)
    in_specs += [_resident_spec(a, l) for a, l in stage_resident]
    return pl.pallas_call(
        functools.partial(_ffn_then_stage_kernel, stage_steps, layer, len(stage_rows),
                          len(stage_resident)),
        name=name,
        grid=(N_TILES + 1,),
        in_specs=in_specs,
        out_specs=pl.BlockSpec((ROW_TILE, D_MODEL), lambda s: (jnp.maximum(s - 1, 0), 0)),
        out_shape=jax.ShapeDtypeStruct((TOKENS, D_MODEL), F32),
        scratch_shapes=[pltpu.VMEM((ROW_TILE, D_MODEL), F32)] + _ffn_weight_scratch()
        + list(scratch_shapes),
        compiler_params=_compiler_params(),
    )(x, *[a for a, _ in stage_rows], *ffn, *[a for a, _ in stage_resident])


def _vecs(v):
    return v.reshape(v.shape[0], 1, v.shape[1])


def kernel(x, p, ff1_pre_g, ff1_w_gate, ff1_w_up, ff1_w_down, ff1_post_g, mix_pre_g, mix_post_g, ff2_pre_g, ff2_w_gate, ff2_w_up, ff2_w_down, ff2_post_g, ple_gate_norm_g, ple_w_gate, ple_w_proj, ple_post_g, a_w_in, a_v_norm_g, a_v_norm_b, a_w_s, a_b_s, a_w_out, b_w_grp, b_scale, c_w_pw1, c_w_dw, c_b_dw, c_norm_g, c_norm_b, c_w_pw2, d_w_in, d_w_conv, d_w_out):
    bf = lambda w: w.astype(BF16)
    ple_w_gate, ple_w_proj = bf(ple_w_gate), bf(ple_w_proj)
    a_w_in, a_w_out, b_w_grp = bf(a_w_in), bf(a_w_out), bf(b_w_grp)
    c_w_pw1, c_w_pw2, d_w_in, d_w_out = bf(c_w_pw1), bf(c_w_pw2), bf(d_w_in), bf(d_w_out)

    ff1_pre_g, ff1_post_g, ff2_pre_g, ff2_post_g = map(
        _vecs, (ff1_pre_g, ff1_post_g, ff2_pre_g, ff2_post_g))
    mix_pre_g, mix_post_g = _vecs(mix_pre_g), _vecs(mix_post_g)
    ple_gate_norm_g, ple_post_g = _vecs(ple_gate_norm_g), _vecs(ple_post_g)
    a_v_norm_g, a_v_norm_b, b_scale = _vecs(a_v_norm_g), _vecs(a_v_norm_b), _vecs(b_scale)
    c_b_dw, c_norm_g, c_norm_b = _vecs(c_b_dw), _vecs(c_norm_g), _vecs(c_norm_b)
    a_b_s = a_b_s.reshape(a_b_s.shape[0], A_GROUPS, CHUNK, 1)

    h = x.reshape(TOKENS, D_MODEL)
    p = p.reshape(DEPTH, TOKENS, PLE_DIM)
    ffn1 = (ff1_pre_g, ff1_w_gate, ff1_w_up, ff1_w_down, ff1_post_g)
    ffn2 = (ff2_pre_g, ff2_w_gate, ff2_w_up, ff2_w_down, ff2_post_g)
    for i in range(DEPTH):
        m, j = i % N_MIXERS, i // N_MIXERS
        pre, post = (mix_pre_g, i), (mix_post_g, i)
        if m == 0:
            h = _ffn_call(f"ffn1_l{i}", h, ffn1, i)
            h = _call(_gmlp_kernel, f"gmlp_l{i}", h,
                      [pre, (a_w_in, j), (a_v_norm_g, j), (a_v_norm_b, j), (a_w_s, j),
                       (a_b_s, j), (a_w_out, j), post],
                      [pltpu.VMEM((ROW_TILE, 2 * A_HALF), F32)])
        elif m == 1:
            h = _fused_call(_pool_steps, f"ffn1_pool_l{i}", h, ffn1, i,
                            [pre, (b_w_grp, j), (b_scale, j), post],
                            scratch_shapes=[pltpu.VMEM((POOL_HALO + ROW_TILE, D_MODEL), F32)])
        elif m == 2:
            h = _fused_call(_conformer_steps, f"ffn1_conformer_l{i}", h, ffn1, i,
                            [pre, (c_w_pw1, j), (c_w_dw, j), (c_b_dw, j), (c_norm_g, j),
                             (c_norm_b, j), (c_w_pw2, j), post],
                            scratch_shapes=[pltpu.VMEM((CONV_HALO + ROW_TILE, D_MODEL), F32)])
        else:
            h = _fused_call(_short_conv_steps, f"ffn1_shortconv_l{i}", h, ffn1, i,
                            [pre, (d_w_in, j), (d_w_conv, j), (d_w_out, j), post],
                            scratch_shapes=[pltpu.VMEM((SHORT_HALO + ROW_TILE, D_MODEL), F32)])
        h = _fused_call(_ple_steps, f"ffn2_ple_l{i}", h, ffn2, i,
                        [(ple_gate_norm_g, i), (ple_w_gate, i), (ple_w_proj, i), (ple_post_g, i)],
                        stage_rows=[(p, i)])
    return h.reshape(BATCH, SEQ, D_MODEL)
```

```python
import collections
import functools

import jax
import jax.numpy as jnp
from jax import lax
from jax.experimental import pallas as pl
from jax.experimental.pallas import tpu as pltpu

D_MODEL = 1024
BATCH = 8
SEQ = 4096
DEPTH = 4
N_MIXERS = 4
D_FF = 2816
PLE_DIM = 256
CHUNK = 128
A_HALF = 3 * D_MODEL
A_GROUPS = A_HALF // 256
A_GROUP_WIDTH = A_HALF // A_GROUPS
POOL_WINDOWS = (2, 4, 8, 16)
POOL_GROUP = D_MODEL // len(POOL_WINDOWS)
CONV_WIDTH = 31
SHORT_CONV_WIDTH = 3
EPS = 1e-6

TOKENS = BATCH * SEQ
ROW_TILE = 512
N_TILES = TOKENS // ROW_TILE
TILES_PER_SEQ = SEQ // ROW_TILE
FF_CHUNK = 256
SUBLANES = 8
LANES = 128
POOL_HALO = 16
CONV_HALO = 32
SHORT_HALO = 8
VMEM_LIMIT_BYTES = 56 * 1024 * 1024

BF16 = jnp.bfloat16
F32 = jnp.float32


def _dot(a, b):
    return jnp.dot(a, b, preferred_element_type=F32)


def _rms(x, g):
    return x * lax.rsqrt(jnp.mean(x * x, axis=-1, keepdims=True) + EPS) * g


def _layer_norm(x, g, b):
    mu = jnp.mean(x, axis=-1, keepdims=True)
    xc = x - mu
    return xc * lax.rsqrt(jnp.mean(xc * xc, axis=-1, keepdims=True) + EPS) * g + b


def _swiglu_tile(xn, wg_ref, wu_ref, wd_ref):
    acc = jnp.zeros((xn.shape[0], D_MODEL), F32)
    for c in range(D_FF // FF_CHUNK):
        cols = slice(c * FF_CHUNK, (c + 1) * FF_CHUNK)
        gate = _dot(xn, wg_ref[:, cols])
        up = _dot(xn, wu_ref[:, cols])
        hid = (gate * jax.nn.sigmoid(gate) * up).astype(BF16)
        acc = acc + _dot(hid, wd_ref[cols, :])
    return acc


def _gmlp_kernel(x_ref, pre_g, w_in, v_g, v_b, w_s, b_s, w_out, post_g, o_ref, z_scr):
    hn = _rms(x_ref[...], pre_g[...]).astype(BF16)
    for g in range(2 * A_GROUPS):
        cols = slice(g * A_GROUP_WIDTH, (g + 1) * A_GROUP_WIDTH)
        z_scr[:, cols] = jax.nn.gelu(_dot(hn, w_in[:, cols]))
    vn = _layer_norm(z_scr[:, A_HALF:2 * A_HALF], v_g[...], v_b[...]).astype(BF16)
    row = lax.broadcasted_iota(jnp.int32, (CHUNK, CHUNK), 0)
    col = lax.broadcasted_iota(jnp.int32, (CHUNK, CHUNK), 1)
    causal = col <= row
    acc = jnp.zeros((ROW_TILE, D_MODEL), F32)
    for g in range(A_GROUPS):
        cols = slice(g * A_GROUP_WIDTH, (g + 1) * A_GROUP_WIDTH)
        ws = jnp.where(causal, w_s[g], 0.0).astype(BF16)
        bias = b_s[g]
        sv = jnp.concatenate(
            [_dot(ws, vn[n * CHUNK:(n + 1) * CHUNK, cols]) + bias
             for n in range(ROW_TILE // CHUNK)], axis=0)
        acc = acc + _dot((z_scr[:, cols] * sv).astype(BF16), w_out[cols, :])
    o_ref[...] = x_ref[...] + _rms(acc, post_g[...])


Lag = collections.namedtuple("Lag", ["tile_in_seq", "next_is_seq_start"])


def _zero_after(value):
    bits = pltpu.bitcast(value[0:SUBLANES, 0:LANES], jnp.uint32)
    bits = lax.shift_right_logical(lax.shift_right_logical(bits, jnp.uint32(16)), jnp.uint32(16))
    return pltpu.bitcast(bits, F32)[0:1, 0:1]


def _after(value, pace):
    if not pace:
        return value
    return value + _zero_after(pace[-1]).astype(value.dtype)


def _interleave(*gens):
    live = list(gens)
    while live:
        for g in list(live):
            try:
                next(g)
            except StopIteration:
                live.remove(g)


def _ffn_steps(x_ref, pre_g, wg, wu, wd, post_g, ffn_pace, other_pace, result):
    xn = _rms(x_ref[...], pre_g[...]).astype(BF16)
    acc = jnp.zeros((ROW_TILE, D_MODEL), F32)
    yield
    for c in range(D_FF // FF_CHUNK):
        cols = slice(c * FF_CHUNK, (c + 1) * FF_CHUNK)
        xc = _after(xn, other_pace)
        gate = _dot(xc, wg[:, cols])
        up = _dot(xc, wu[:, cols])
        hid = (gate * jax.nn.sigmoid(gate) * up).astype(BF16)
        acc = acc + _dot(hid, wd[cols, :])
        ffn_pace.append(acc)
        yield
    result.append(x_ref[...] + 0.5 * _rms(acc, post_g[...]))


def _causal_taps(ext_val, halo, w_ref, cols, init):
    n_taps = w_ref.shape[0]
    first = halo - (n_taps - 1)
    ext_rows = halo + ROW_TILE
    out = init
    for r in range(SUBLANES):
        taps = [k for k in range(n_taps) if (first + k) % SUBLANES == r]
        if not taps:
            continue
        shifted = ext_val if r == 0 else pltpu.roll(ext_val, ext_rows - r, 0)
        for k in taps:
            base = first + k - r
            out = out + shifted[base:base + ROW_TILE, :] * w_ref[k:k + 1, cols]
    return out


def _pool_steps(h_ref, lag, pre_g, w_grp, scale, post_g, ext, ffn_pace, own_pace, result):
    ext[POOL_HALO:POOL_HALO + ROW_TILE, :] = _rms(h_ref[...], pre_g[...])
    yield
    pos = lag.tile_in_seq * ROW_TILE + lax.broadcasted_iota(jnp.int32, (ROW_TILE, 1), 0)
    outs = []
    for g, w in enumerate(POOL_WINDOWS):
        cols = slice(g * POOL_GROUP, (g + 1) * POOL_GROUP)
        total = _after(ext[:, cols], ffn_pace)
        span = 1
        while span < w:
            total = total + pltpu.roll(total, span, 0)
            span *= 2
        total = total[POOL_HALO:POOL_HALO + ROW_TILE, :]
        count = jnp.minimum(pos + 1, w).astype(F32)
        pooled = (total / count - ext[POOL_HALO:POOL_HALO + ROW_TILE, cols]).astype(BF16)
        outs.append(_dot(pooled, w_grp[g]))
        own_pace.append(outs[-1])
        yield
    y = jnp.concatenate(outs, axis=-1) * scale[...]
    tail = ext[ROW_TILE:ROW_TILE + POOL_HALO, :]
    ext[0:POOL_HALO, :] = jnp.where(lag.next_is_seq_start, 0.0, tail)
    result.append(h_ref[...] + _rms(y, post_g[...]))


def _conformer_steps(h_ref, lag, pre_g, w_pw1, w_dw, b_dw, n_g, n_b, w_pw2, post_g, ext,
                     ffn_pace, own_pace, result):
    hn = _rms(h_ref[...], pre_g[...]).astype(BF16)
    a = _dot(hn, w_pw1[:, 0:D_MODEL])
    gate = _dot(hn, w_pw1[:, D_MODEL:2 * D_MODEL])
    ext[CONV_HALO:CONV_HALO + ROW_TILE, :] = a * jax.nn.sigmoid(gate)
    yield
    conv_cols = []
    for cb in range(D_MODEL // LANES):
        cols = slice(cb * LANES, (cb + 1) * LANES)
        bias = jnp.zeros((ROW_TILE, LANES), F32) + b_dw[:, cols]
        conv = _causal_taps(_after(ext[:, cols], ffn_pace), CONV_HALO, w_dw, cols, bias)
        conv_cols.append(conv)
        own_pace.append(conv)
        yield
    z = _layer_norm(_after(jnp.concatenate(conv_cols, axis=-1), ffn_pace), n_g[...], n_b[...])
    z = (z * jax.nn.sigmoid(z)).astype(BF16)
    yield
    y = _dot(z, w_pw2[...])
    yield
    tail = ext[ROW_TILE:ROW_TILE + CONV_HALO, :]
    ext[0:CONV_HALO, :] = jnp.where(lag.next_is_seq_start, 0.0, tail)
    result.append(h_ref[...] + _rms(y, post_g[...]))


def _short_conv_steps(h_ref, lag, pre_g, w_in, w_conv, w_out, post_g, ext,
                      ffn_pace, own_pace, result):
    hn = _rms(h_ref[...], pre_g[...]).astype(BF16)
    cg = _dot(hn, w_in[:, D_MODEL:2 * D_MODEL])
    xv = _dot(hn, w_in[:, 2 * D_MODEL:3 * D_MODEL])
    ext[SHORT_HALO:SHORT_HALO + ROW_TILE, :] = cg * xv
    yield
    bg = _dot(_after(hn, ffn_pace), w_in[:, 0:D_MODEL])
    own_pace.append(bg)
    yield
    gated_cols = []
    for cb in range(D_MODEL // LANES):
        cols = slice(cb * LANES, (cb + 1) * LANES)
        conv = _causal_taps(_after(ext[:, cols], ffn_pace), SHORT_HALO, w_conv, cols,
                            jnp.zeros((ROW_TILE, LANES), F32))
        gated_cols.append((bg[:, cols] * conv).astype(BF16))
        own_pace.append(conv)
        yield
    y = _dot(jnp.concatenate(gated_cols, axis=-1), w_out[...])
    yield
    tail = ext[ROW_TILE:ROW_TILE + SHORT_HALO, :]
    ext[0:SHORT_HALO, :] = jnp.where(lag.next_is_seq_start, 0.0, tail)
    result.append(h_ref[...] + _rms(y, post_g[...]))


def _ple_steps(h_ref, lag, p_ref, gn_g, w_gate, w_proj, post_g, ffn_pace, own_pace, result):
    gate_in = _rms(h_ref[...], gn_g[...]).astype(BF16)
    gate_lin = _dot(gate_in, w_gate[...])
    yield
    proj = _dot(p_ref[...].astype(BF16), w_proj[...])
    e = proj * jax.nn.sigmoid(_after(gate_lin, ffn_pace))
    own_pace.append(e)
    yield
    result.append(h_ref[...] + _rms(_after(e, ffn_pace), post_g[...]))


FFN_WIDE_ROWS = 128
FFN_NARROW_ROWS = 256
FFN_STAGING_SLOTS = 3
N_FFN_SCRATCH = 7


def _ffn_weight_scratch():
    return [pltpu.VMEM((D_MODEL, D_FF), BF16),
            pltpu.VMEM((D_MODEL, D_FF), BF16),
            pltpu.VMEM((D_FF, D_MODEL), BF16),
            pltpu.VMEM((FFN_STAGING_SLOTS, FFN_WIDE_ROWS, D_FF), F32),
            pltpu.VMEM((FFN_STAGING_SLOTS, FFN_NARROW_ROWS, D_MODEL), F32),
            pltpu.SemaphoreType.DMA((FFN_STAGING_SLOTS,)),
            pltpu.SemaphoreType.DMA((FFN_STAGING_SLOTS,))]


def _fetch_steps(pairs, layer, staging, sem, priority):
    slots, rows = staging.shape[0], staging.shape[1]
    assert all(dst.shape[0] % rows == 0 for _, dst in pairs)
    chunks = [(src, dst, i) for src, dst in pairs for i in range(dst.shape[0] // rows)]
    ahead = slots - 1

    def copy(n):
        src, _, i = chunks[n]
        return pltpu.make_async_copy(src.at[layer, pl.ds(i * rows, rows), :],
                                     staging.at[n % slots], sem.at[n % slots])

    for n in range(min(ahead, len(chunks))):
        copy(n).start(priority)
    yield
    for n, (_, dst, i) in enumerate(chunks):
        if n + ahead < len(chunks):
            copy(n + ahead).start(priority)
        copy(n).wait()
        dst[pl.ds(i * rows, rows), :] = staging[n % slots].astype(BF16)
        yield


def _fetch_ffn_weights(layer, wg_hbm, wu_hbm, wd_hbm, ffn_scratch):
    wg, wu, wd, wide, narrow, sem_wide, sem_narrow = ffn_scratch
    _interleave(_fetch_steps([(wg_hbm, wg), (wu_hbm, wu)], layer, wide, sem_wide, 0),
                _fetch_steps([(wd_hbm, wd)], layer, narrow, sem_narrow, 1))


def _ffn_kernel(layer, x_ref, pre_g, wg_hbm, wu_hbm, wd_hbm, post_g, o_ref, *ffn_scratch):
    @pl.when(pl.program_id(0) == 0)
    def _():
        _fetch_ffn_weights(layer, wg_hbm, wu_hbm, wd_hbm, ffn_scratch)

    wg, wu, wd = ffn_scratch[:3]
    x = x_ref[...]
    xn = _rms(x, pre_g[...]).astype(BF16)
    f = _swiglu_tile(xn, wg, wu, wd)
    o_ref[...] = x + 0.5 * _rms(f, post_g[...])


N_FFN_INPUTS = 5


def _ffn_then_stage_kernel(stage_steps, layer, n_stage_rows, n_stage_resident, x_ref, *refs):
    stage_rows, refs = refs[:n_stage_rows], refs[n_stage_rows:]
    (pre_g, wg_hbm, wu_hbm, wd_hbm, post_g), refs = refs[:N_FFN_INPUTS], refs[N_FFN_INPUTS:]
    stage_res, refs = refs[:n_stage_resident], refs[n_stage_resident:]
    o_ref, hbuf, refs = refs[0], refs[1], refs[2:]
    ffn_scratch, scratch = refs[:N_FFN_SCRATCH], refs[N_FFN_SCRATCH:]
    wg, wu, wd = ffn_scratch[:3]
    s = pl.program_id(0)

    @pl.when(s == 0)
    def _():
        _fetch_ffn_weights(layer, wg_hbm, wu_hbm, wd_hbm, ffn_scratch)
        hbuf[...] = jnp.zeros(hbuf.shape, F32)
        for scr in scratch:
            scr[...] = jnp.zeros(scr.shape, scr.dtype)

    lag = Lag(tile_in_seq=(s + TILES_PER_SEQ - 1) % TILES_PER_SEQ,
              next_is_seq_start=s % TILES_PER_SEQ == 0)
    ffn_pace, stage_pace, ffn_out, stage_out = [], [], [], []
    _interleave(
        _ffn_steps(x_ref, pre_g, wg, wu, wd, post_g, ffn_pace, stage_pace, ffn_out),
        stage_steps(hbuf, lag, *stage_rows, *stage_res, *scratch, ffn_pace, stage_pace,
                    stage_out))
    o_ref[...] = stage_out[0]
    hbuf[...] = ffn_out[0]


def _row_spec(width):
    return pl.BlockSpec((ROW_TILE, width), lambda s: (s, 0))


def _resident_spec(stacked, layer):
    zeros = (0,) * (stacked.ndim - 1)
    return pl.BlockSpec((None,) + stacked.shape[1:], lambda s: (layer,) + zeros,
                        pipeline_mode=pl.Buffered(1))


def _compiler_params():
    return pltpu.CompilerParams(dimension_semantics=("arbitrary",),
                                vmem_limit_bytes=VMEM_LIMIT_BYTES)


def _call(body, name, x, resident_inputs, scratch_shapes=()):
    return pl.pallas_call(
        body,
        name=name,
        grid=(N_TILES,),
        in_specs=[_row_spec(D_MODEL)] + [_resident_spec(a, l) for a, l in resident_inputs],
        out_specs=_row_spec(D_MODEL),
        out_shape=jax.ShapeDtypeStruct((TOKENS, D_MODEL), F32),
        scratch_shapes=list(scratch_shapes),
        compiler_params=_compiler_params(),
    )(x, *[a for a, _ in resident_inputs])


def _ffn_specs(ffn, layer):
    pre_g, _, _, _, post_g = ffn
    hbm = pl.BlockSpec(memory_space=pl.ANY)
    return [_resident_spec(pre_g, layer), hbm, hbm, hbm, _resident_spec(post_g, layer)]


def _ffn_call(name, x, ffn, layer):
    return pl.pallas_call(
        functools.partial(_ffn_kernel, layer),
        name=name,
        grid=(N_TILES,),
        in_specs=[_row_spec(D_MODEL)] + _ffn_specs(ffn, layer),
        out_specs=_row_spec(D_MODEL),
        out_shape=jax.ShapeDtypeStruct((TOKENS, D_MODEL), F32),
        scratch_shapes=_ffn_weight_scratch(),
        compiler_params=_compiler_params(),
    )(x, *ffn)


def _fused_call(stage_steps, name, x, ffn, layer, stage_resident, stage_rows=(),
                scratch_shapes=()):
    in_specs = [pl.BlockSpec((ROW_TILE, D_MODEL), lambda s: (jnp.minimum(s, N_TILES - 1), 0))]
    for a, row_layer in stage_rows:
        in_specs.append(pl.BlockSpec(
            (None, ROW_TILE, a.shape[-1]),
            lambda s, row_layer=row_layer: (row_layer, jnp.maximum(s - 1, 0), 0)))
    in_specs += _ffn_specs(ffn, layer)
    in_specs += [_resident_spec(a, l) for a, l in stage_resident]
    return pl.pallas_call(
        functools.partial(_ffn_then_stage_kernel, stage_steps, layer, len(stage_rows),
                          len(stage_resident)),
        name=name,
        grid=(N_TILES + 1,),
        in_specs=in_specs,
        out_specs=pl.BlockSpec((ROW_TILE, D_MODEL), lambda s: (jnp.maximum(s - 1, 0), 0)),
        out_shape=jax.ShapeDtypeStruct((TOKENS, D_MODEL), F32),
        scratch_shapes=[pltpu.VMEM((ROW_TILE, D_MODEL), F32)] + _ffn_weight_scratch()
        + list(scratch_shapes),
        compiler_params=_compiler_params(),
    )(x, *[a for a, _ in stage_rows], *ffn, *[a for a, _ in stage_resident])


def _vecs(v):
    return v.reshape(v.shape[0], 1, v.shape[1])


def kernel(x, p, ff1_pre_g, ff1_w_gate, ff1_w_up, ff1_w_down, ff1_post_g, mix_pre_g, mix_post_g, ff2_pre_g, ff2_w_gate, ff2_w_up, ff2_w_down, ff2_post_g, ple_gate_norm_g, ple_w_gate, ple_w_proj, ple_post_g, a_w_in, a_v_norm_g, a_v_norm_b, a_w_s, a_b_s, a_w_out, b_w_grp, b_scale, c_w_pw1, c_w_dw, c_b_dw, c_norm_g, c_norm_b, c_w_pw2, d_w_in, d_w_conv, d_w_out):
    bf = lambda w: w.astype(BF16)
    ple_w_gate, ple_w_proj = bf(ple_w_gate), bf(ple_w_proj)
    a_w_in, a_w_out, b_w_grp = bf(a_w_in), bf(a_w_out), bf(b_w_grp)
    c_w_pw1, c_w_pw2, d_w_in, d_w_out = bf(c_w_pw1), bf(c_w_pw2), bf(d_w_in), bf(d_w_out)

    ff1_pre_g, ff1_post_g, ff2_pre_g, ff2_post_g = map(
        _vecs, (ff1_pre_g, ff1_post_g, ff2_pre_g, ff2_post_g))
    mix_pre_g, mix_post_g = _vecs(mix_pre_g), _vecs(mix_post_g)
    ple_gate_norm_g, ple_post_g = _vecs(ple_gate_norm_g), _vecs(ple_post_g)
    a_v_norm_g, a_v_norm_b, b_scale = _vecs(a_v_norm_g), _vecs(a_v_norm_b), _vecs(b_scale)
    c_b_dw, c_norm_g, c_norm_b = _vecs(c_b_dw), _vecs(c_norm_g), _vecs(c_norm_b)
    a_b_s = a_b_s.reshape(a_b_s.shape[0], A_GROUPS, CHUNK, 1)

    h = x.reshape(TOKENS, D_MODEL)
    p = p.reshape(DEPTH, TOKENS, PLE_DIM)
    ffn1 = (ff1_pre_g, ff1_w_gate, ff1_w_up, ff1_w_down, ff1_post_g)
    ffn2 = (ff2_pre_g, ff2_w_gate, ff2_w_up, ff2_w_down, ff2_post_g)
    for i in range(DEPTH):
        m, j = i % N_MIXERS, i // N_MIXERS
        pre, post = (mix_pre_g, i), (mix_post_g, i)
        if m == 0:
            h = _ffn_call(f"ffn1_l{i}", h, ffn1, i)
            h = _call(_gmlp_kernel, f"gmlp_l{i}", h,
                      [pre, (a_w_in, j), (a_v_norm_g, j), (a_v_norm_b, j), (a_w_s, j),
                       (a_b_s, j), (a_w_out, j), post],
                      [pltpu.VMEM((ROW_TILE, 2 * A_HALF), F32)])
        elif m == 1:
            h = _fused_call(_pool_steps, f"ffn1_pool_l{i}", h, ffn1, i,
                            [pre, (b_w_grp, j), (b_scale, j), post],
                            scratch_shapes=[pltpu.VMEM((POOL_HALO + ROW_TILE, D_MODEL), F32)])
        elif m == 2:
            h = _fused_call(_conformer_steps, f"ffn1_conformer_l{i}", h, ffn1, i,
                            [pre, (c_w_pw1, j), (c_w_dw, j), (c_b_dw, j), (c_norm_g, j),
                             (c_norm_b, j), (c_w_pw2, j), post],
                            scratch_shapes=[pltpu.VMEM((CONV_HALO + ROW_TILE, D_MODEL), F32)])
        else:
            h = _fused_call(_short_conv_steps, f"ffn1_shortconv_l{i}", h, ffn1, i,
                            [pre, (d_w_in, j), (d_w_conv, j), (d_w_out, j), post],
                            scratch_shapes=[pltpu.VMEM((SHORT_HALO + ROW_TILE, D_MODEL), F32)])
        h = _fused_call(_ple_steps, f"ffn2_ple_l{i}", h, ffn2, i,
                        [(ple_gate_norm_g, i), (ple_w_gate, i), (ple_w_proj, i), (ple_post_g, i)],
                        stage_rows=[(p, i)])
    return h.reshape(BATCH, SEQ, D_MODEL)
```
